```python
import math
import jax
import jax.numpy as jnp
from jax import lax
import numpy as np

D_MODEL = 1024
BATCH = 8
SEQ = 4096
DEPTH = 4

GRID_W = 64
CTX_LEN = 256
N_MIXERS = 2
HEAD_DIM = 64
RWKV_HEADS = D_MODEL // HEAD_DIM
DECAY_LORA = 64
AAA_LORA = 64
MV_LORA = 32
GATE_LORA = 160
N_MIX_COEF = 6
DIFF_HEADS = D_MODEL // (2 * HEAD_DIM)
D_FF = 4 * D_MODEL
Q_BLOCK = 128
ROPE_THETA = 10000.0
ROPE_PAIRS = HEAD_DIM // 4
NORM_EPS = 1e-6
SUBLN_EPS = 1e-5
GN_EPS = 64e-5
N_RWKV = (DEPTH + 1) // 2
N_DIFF = DEPTH // 2
N_VRES = N_RWKV - 1

kernel_name = 'hybrid_rwkv7_diffattn_dit'


def _rmsnorm(x, g, eps=NORM_EPS):
    xf = x.astype(jnp.float32)
    y = xf * lax.rsqrt(jnp.mean(jnp.square(xf), axis=-1, keepdims=True) + eps)
    return (y * g).astype(x.dtype)


def _modulate(h, shift, scale):
    return h * (1 + scale) + shift


def _sqrelu_mlp(h, w1, w2):
    return jnp.square(jax.nn.relu(h @ w1)) @ w2


def _rope_tables(L):
    rows = L // GRID_W
    row = jnp.repeat(jnp.arange(rows, dtype=jnp.int32), GRID_W)
    col = jnp.tile(jnp.arange(GRID_W, dtype=jnp.int32), rows)
    inv = ROPE_THETA ** (-jnp.arange(ROPE_PAIRS, dtype=jnp.float32) / ROPE_PAIRS)
    ang_r = row.astype(jnp.float32)[:, None] * inv[None, :]
    ang_c = col.astype(jnp.float32)[:, None] * inv[None, :]
    return (jnp.cos(ang_r), jnp.sin(ang_r), jnp.cos(ang_c), jnp.sin(ang_c))


def _rope_half(x, cos, sin):
    x1, x2 = jnp.split(x, 2, axis=-1)
    return jnp.concatenate([x1 * cos - x2 * sin, x1 * sin + x2 * cos], axis=-1)


def _rope2d(x, rope):
    cos_r, sin_r, cos_c, sin_c = rope
    xr, xc = jnp.split(x, 2, axis=-1)
    return jnp.concatenate([_rope_half(xr, cos_r, sin_r), _rope_half(xc, cos_c, sin_c)], axis=-1).astype(x.dtype)


def _token_shift(h):
    z = jnp.zeros_like(h[:, :1])
    prev = jnp.concatenate([z, h[:, :-1]], axis=1)
    nxt = jnp.concatenate([h[:, 1:], z], axis=1)
    return 0.5 * (prev + nxt) - h


def _rwkv_features(h, mix, w_rkv, w0, w1, w2, a0, a1, a2, g1, g2, k_k, k_a, vres):
    B, T, _ = h.shape
    xx = _token_shift(h)
    xr, xw, xk, xv, xa, xg = (h + xx * mix[m] for m in range(N_MIX_COEF))
    r, k, v = jnp.einsum('sbtd,sde->sbte', jnp.stack([xr, xk, xv]), w_rkv)
    decays = tuple(
        jnp.exp(-jnp.exp(-jax.nn.softplus(-(w0[d] + jnp.tanh(xw @ w1[d]) @ w2[d]).astype(jnp.float32)) - 0.5))
        for d in range(2))
    gates = tuple(jax.nn.sigmoid(xg @ g1[d]) @ g2[d] for d in range(2))
    a = jax.nn.sigmoid(a0 + (xa @ a1) @ a2)
    if vres is not None:
        v_first, v0, v1, v2 = vres
        v = v + (v_first - v) * jax.nn.sigmoid(v0 + (xv @ v1) @ v2)
    kk = (k * k_k).reshape(B, T, RWKV_HEADS, HEAD_DIM)
    inv_norm = lax.rsqrt(jnp.maximum(jnp.sum(jnp.square(kk.astype(jnp.float32)), -1, keepdims=True), 1e-24))
    kk = (kk * inv_norm.astype(kk.dtype)).reshape(B, T, D_MODEL)
    k = k * (1 + (a - 1) * k_a)
    return (r, k, v, kk, a, decays, gates)


def _wkv_scan(r, decay, k, v, a_vec, b_vec, s0, reverse):
    B, T, _ = r.shape

    def to_steps(t):
        return jnp.moveaxis(t.astype(jnp.float32).reshape(B, T, RWKV_HEADS, HEAD_DIM), 1, 0)

    def step(S, inp):
        r_t, w_t, k_t, v_t, a_t, b_t = inp
        sa = jnp.einsum('bhvk,bhk->bhv', S, a_t)
        S = S * w_t[:, :, None, :] + sa[..., None] * b_t[:, :, None, :] + v_t[..., None] * k_t[:, :, None, :]
        return S, jnp.einsum('bhvk,bhk->bhv', S, r_t)

    s_fin, o = lax.scan(step, s0, tuple(to_steps(t) for t in (r, decay, k, v, a_vec, b_vec)), reverse=reverse)
    return s_fin, jnp.moveaxis(o, 0, 1).reshape(B, T, D_MODEL)


def _group_norm(o, g, b):
    B, T, _ = o.shape
    oh = o.astype(jnp.float32).reshape(B, T, RWKV_HEADS, HEAD_DIM)
    mu = jnp.mean(oh, axis=-1, keepdims=True)
    var = jnp.mean(jnp.square(oh - mu), axis=-1, keepdims=True)
    on = ((oh - mu) * lax.rsqrt(var + GN_EPS)).reshape(B, T, D_MODEL)
    return on * g + b


def _rwkv_output(f, o_f, o_b, r_k, ln_g, ln_b, w_o):
    r, k, v, _, _, _, gates = f
    B, T, _ = r.shape
    hs = (B, T, RWKV_HEADS, HEAD_DIM)
    bonus = (jnp.sum((r * k).reshape(hs) * r_k, axis=-1, keepdims=True) * v.reshape(hs)).reshape(B, T, D_MODEL)
    y_f = (_group_norm(o_f, ln_g, ln_b).astype(r.dtype) + bonus) * gates[0]
    y_b = (_group_norm(o_b, ln_g, ln_b).astype(r.dtype) + bonus) * gates[1]
    return (y_f + y_b) @ w_o


def _scan_dir(f, d, s_init, reverse):
    r, k, v, kk, a, decays, _ = f
    return _wkv_scan(r, decays[d], k, v, -kk, kk * a, s_init, reverse)


def _rwkv_mixer(h_lat, h_ctx, feat_params, out_params, vres_lat, vres_ctx, need_ctx_out):
    f_ctx = _rwkv_features(h_ctx, *feat_params, vres_ctx)
    f_lat = _rwkv_features(h_lat, *feat_params, vres_lat)
    s0 = jnp.zeros((h_lat.shape[0], RWKV_HEADS, HEAD_DIM, HEAD_DIM), jnp.float32)
    s_cf, o_cf = _scan_dir(f_ctx, 0, s0, False)
    _, o_lf = _scan_dir(f_lat, 0, s_cf, False)
    s_cb, o_cb = _scan_dir(f_ctx, 1, s0, True)
    _, o_lb = _scan_dir(f_lat, 1, s_cb, True)
    y_lat = _rwkv_output(f_lat, o_lf.astype(h_lat.dtype), o_lb.astype(h_lat.dtype), *out_params)
    y_ctx = _rwkv_output(f_ctx, o_cf.astype(h_ctx.dtype), o_cb.astype(h_ctx.dtype), *out_params) if need_ctx_out else None
    return y_lat, y_ctx, f_lat[2], f_ctx[2]


def _diff_attention(h_lat, h_ctx, w_qkv, w_o, lq1, lk1, lq2, lk2, subln_g, lambda_init, rope, need_ctx_out):
    B, L, _ = h_lat.shape

    def project(h):
        T = h.shape[1]
        q, k, v = jnp.split(h @ w_qkv, 3, axis=-1)
        q = q.reshape(B, T, DIFF_HEADS, 2, HEAD_DIM).transpose(0, 2, 3, 1, 4)
        k = k.reshape(B, T, DIFF_HEADS, 2, HEAD_DIM).transpose(0, 2, 3, 1, 4)
        v = v.reshape(B, T, DIFF_HEADS, 2 * HEAD_DIM).transpose(0, 2, 1, 3)
        return q, k, v

    q_l, k_l, v_l = project(h_lat)
    _, k_c, v_c = project(h_ctx) if not need_ctx_out else (None, None, None)
    if need_ctx_out:
        q_c, k_c, v_c = project(h_ctx)
    q_l = _rope2d(q_l, rope)
    k_l = _rope2d(k_l, rope)
    lam = (jnp.exp(jnp.sum(lq1 * lk1).astype(jnp.float32)) - jnp.exp(jnp.sum(lq2 * lk2).astype(jnp.float32))
           + lambda_init)
    scale = HEAD_DIM ** -0.5

    def combine(s):
        p = jax.nn.softmax(s.astype(jnp.float32) * scale, axis=-1)
        return p[:, :, 0] - lam * p[:, :, 1]

    def head_out(o):
        T = o.shape[1]
        o = _rmsnorm(o, subln_g, SUBLN_EPS) * (1 - lambda_init)
        return o.reshape(B, T, D_MODEL) @ w_o

    k_all = jnp.concatenate([k_l, k_c], axis=3)
    v_all = jnp.concatenate([v_l, v_c], axis=2)
    nb = L // Q_BLOCK
    q_blocks = q_l.reshape(B, DIFF_HEADS, 2, nb, Q_BLOCK, HEAD_DIM).transpose(3, 0, 1, 2, 4, 5)

    def block(qb):
        a = combine(jnp.einsum('bhsqd,bhskd->bhsqk', qb, k_all))
        return jnp.einsum('bhqk,bhkv->bhqv', a.astype(v_all.dtype), v_all)

    o_l = lax.map(block, q_blocks)
    o_l = o_l.transpose(1, 0, 3, 2, 4).reshape(B, L, DIFF_HEADS, 2 * HEAD_DIM)
    y_lat = head_out(o_l)
    y_ctx = None
    if need_ctx_out:
        a_c = combine(jnp.einsum('bhsqd,bhskd->bhsqk', q_c, k_c))
        o_c = jnp.einsum('bhqk,bhkv->bhqv', a_c.astype(v_c.dtype), v_c).transpose(0, 2, 1, 3)
        y_ctx = head_out(o_c)
    return y_lat, y_ctx


def setup_inputs(seed: int = 0) -> dict:
    key = jax.random.key(seed)
    ks = iter(jax.random.split(key, 48))
    D = D_MODEL

    def nrm(shape, s):
        return jax.random.normal(next(ks), shape, jnp.float32) * s

    def uni(shape, lo, hi):
        return jax.random.uniform(next(ks), shape, jnp.float32, lo, hi)

    return {
        'x': nrm((BATCH, SEQ, D), 1.0),
        'c': nrm((BATCH, D), 1.0),
        'ctx': nrm((BATCH, CTX_LEN, D), 1.0),
        'c_ctx': nrm((D,), 1.0),
        'ada_w': nrm((DEPTH, D, 6 * D), 0.5 * D ** -0.5),
        'ada_b': nrm((DEPTH, 6 * D), 0.01),
        'norm_g': 1.0 + nrm((DEPTH, 2, D), 0.02),
        'final_g': 1.0 + nrm((D,), 0.02),
        'rw_mix': uni((N_RWKV, N_MIX_COEF, D), 0.0, 1.0),
        'rw_w_rkv': nrm((N_RWKV, 3, D, D), D ** -0.5),
        'rw_w0': uni((N_RWKV, 2, D), -6.0, -1.0),
        'rw_w1': nrm((N_RWKV, 2, D, DECAY_LORA), D ** -0.5),
        'rw_w2': nrm((N_RWKV, 2, DECAY_LORA, D), 0.1 * DECAY_LORA ** -0.5),
        'rw_a0': nrm((N_RWKV, D), 0.1),
        'rw_a1': nrm((N_RWKV, D, AAA_LORA), D ** -0.5),
        'rw_a2': nrm((N_RWKV, AAA_LORA, D), 0.1 * AAA_LORA ** -0.5),
        'rw_g1': nrm((N_RWKV, 2, D, GATE_LORA), D ** -0.5),
        'rw_g2': nrm((N_RWKV, 2, GATE_LORA, D), GATE_LORA ** -0.5),
        'rw_kk': 0.85 + nrm((N_RWKV, D), 0.02),
        'rw_ka': 1.0 + nrm((N_RWKV, D), 0.02),
        'rw_rk': nrm((N_RWKV, RWKV_HEADS, HEAD_DIM), 0.1),
        'rw_ln_g': 1.0 + nrm((N_RWKV, D), 0.02),
        'rw_ln_b': nrm((N_RWKV, D), 0.01),
        'rw_w_o': nrm((N_RWKV, D, D), D ** -0.5),
        'rw_v0': 1.0 + nrm((N_VRES, D), 0.02),
        'rw_v1': nrm((N_VRES, D, MV_LORA), D ** -0.5),
        'rw_v2': nrm((N_VRES, MV_LORA, D), 0.1 * MV_LORA ** -0.5),
        'da_w_qkv': nrm((N_DIFF, D, 3 * D), D ** -0.5),
        'da_w_o': nrm((N_DIFF, D, D), D ** -0.5),
        'da_lq1': nrm((N_DIFF, HEAD_DIM), 0.1),
        'da_lk1': nrm((N_DIFF, HEAD_DIM), 0.1),
        'da_lq2': nrm((N_DIFF, HEAD_DIM), 0.1),
        'da_lk2': nrm((N_DIFF, HEAD_DIM), 0.1),
        'da_subln_g': 1.0 + nrm((N_DIFF, 2 * HEAD_DIM), 0.02),
        'mlp_w1': nrm((DEPTH, D, D_FF), D ** -0.5),
        'mlp_w2': nrm((DEPTH, D_FF, D), D_FF ** -0.5),
    }


def reference(x, c, ctx, c_ctx, ada_w, ada_b, norm_g, final_g,
              rw_mix, rw_w_rkv, rw_w0, rw_w1, rw_w2, rw_a0, rw_a1, rw_a2,
              rw_g1, rw_g2, rw_kk, rw_ka, rw_rk, rw_ln_g, rw_ln_b, rw_w_o,
              rw_v0, rw_v1, rw_v2,
              da_w_qkv, da_w_o, da_lq1, da_lk1, da_lq2, da_lk2, da_subln_g,
              mlp_w1, mlp_w2):
    L = x.shape[1]
    rope = _rope_tables(L)
    xc = ctx
    v_first_lat = None
    v_first_ctx = None
    for i in range(DEPTH):
        last = i == DEPTH - 1
        mod_lat = (jax.nn.silu(c) @ ada_w[i] + ada_b[i])[:, None, :]
        mod_ctx = jax.nn.silu(c_ctx) @ ada_w[i] + ada_b[i]
        sh1, sc1, g1, sh2, sc2, g2 = jnp.split(mod_lat, 6, axis=-1)
        csh1, csc1, cg1, csh2, csc2, cg2 = jnp.split(mod_ctx, 6, axis=-1)
        h_lat = _modulate(_rmsnorm(x, norm_g[i, 0]), sh1, sc1)
        h_ctx = _modulate(_rmsnorm(xc, norm_g[i, 0]), csh1, csc1)
        j = i // N_MIXERS
        if i % N_MIXERS == 0:
            vres_lat = None
            vres_ctx = None
            if j > 0:
                vres_lat = (v_first_lat, rw_v0[j - 1], rw_v1[j - 1], rw_v2[j - 1])
                vres_ctx = (v_first_ctx, rw_v0[j - 1], rw_v1[j - 1], rw_v2[j - 1])
            feat_params = (rw_mix[j], rw_w_rkv[j], rw_w0[j], rw_w1[j], rw_w2[j], rw_a0[j], rw_a1[j], rw_a2[j],
                           rw_g1[j], rw_g2[j], rw_kk[j], rw_ka[j])
            out_params = (rw_rk[j], rw_ln_g[j], rw_ln_b[j], rw_w_o[j])
            y_lat, y_ctx, v_lat, v_ctx = _rwkv_mixer(h_lat, h_ctx, feat_params, out_params,
                                                     vres_lat, vres_ctx, not last)
            if j == 0:
                v_first_lat = v_lat
                v_first_ctx = v_ctx
        else:
            lambda_init = 0.8 - 0.6 * math.exp(-0.3 * i)
            y_lat, y_ctx = _diff_attention(h_lat, h_ctx, da_w_qkv[j], da_w_o[j], da_lq1[j], da_lk1[j],
                                           da_lq2[j], da_lk2[j], da_subln_g[j], lambda_init, rope, not last)
        x = x + g1 * y_lat
        h = _modulate(_rmsnorm(x, norm_g[i, 1]), sh2, sc2)
        x = x + g2 * _sqrelu_mlp(h, mlp_w1[i], mlp_w2[i])
        if not last:
            xc = xc + cg1 * y_ctx
            hc = _modulate(_rmsnorm(xc, norm_g[i, 1]), csh2, csc2)
            xc = xc + cg2 * _sqrelu_mlp(hc, mlp_w1[i], mlp_w2[i])
    return _rmsnorm(x, final_g)
```

```python
import functools
import math

import jax
import jax.numpy as jnp
from jax import lax
from jax.experimental import pallas as pl
from jax.experimental.pallas import tpu as pltpu

D_MODEL = 1024
HEAD_DIM = 64
LANES = 128
N_SLAB = D_MODEL // LANES
D_FF = 4 * D_MODEL
GRID_W = 64
ROPE_THETA = 10000.0
ROPE_PAIRS = HEAD_DIM // 4
NORM_EPS = 1e-6
SUBLN_EPS = 1e-5
GN_EPS = 64e-5
CHUNK = 64
SCAN_BLOCK = 256
MOD_ROWS = 8
VMEM_LIMIT = 56 * 1024 * 1024

F32 = jnp.float32
BF16 = jnp.bfloat16


def _cparams(sem):
    return pltpu.CompilerParams(dimension_semantics=sem, vmem_limit_bytes=VMEM_LIMIT)


def _const_spec(shape):
    nd = len(shape)
    return pl.BlockSpec(shape, lambda *_: (0,) * nd, pipeline_mode=pl.Buffered(1))


def _dot(a, b):
    return jnp.dot(a, b, preferred_element_type=F32)


def _dot_nt(a, b):
    return lax.dot_general(a, b, (((1,), (1,)), ((), ())), preferred_element_type=F32)


def _split2(x):
    hi = x.astype(BF16)
    lo = (x - hi.astype(F32)).astype(BF16)
    return hi, lo


def _split3(x):
    hi = x.astype(BF16)
    r1 = x - hi.astype(F32)
    mid = r1.astype(BF16)
    lo = (r1 - mid.astype(F32)).astype(BF16)
    return hi, mid, lo


def _rms_mod(x, g, shift, scale):
    y = x * lax.rsqrt(jnp.mean(x * x, axis=-1, keepdims=True) + NORM_EPS) * g
    return y * (1.0 + scale) + shift


def _head_sum_mat():
    r = lax.broadcasted_iota(jnp.int32, (LANES, LANES), 0) // HEAD_DIM
    c = lax.broadcasted_iota(jnp.int32, (LANES, LANES), 1) // HEAD_DIM
    return (r == c).astype(BF16)


def _head_sum(x, ones_bd2):
    hi, lo = _split2(x)
    return _dot(jnp.concatenate([hi, lo], axis=1), ones_bd2)


def _ada_kernel(cc_ref, w_ref, b_ref, o_ref):
    cc = cc_ref[...]
    s = cc * jax.nn.sigmoid(cc)
    o_ref[0] = jnp.dot(s, w_ref[0], preferred_element_type=F32,
                       precision=lax.Precision.HIGHEST) + b_ref[0]


def _ada_table(cc, ada_w, ada_b):
    depth, d, n = ada_w.shape
    rows = cc.shape[0]
    tn = 1536
    return pl.pallas_call(
        _ada_kernel,
        grid=(depth, n // tn),
        in_specs=[pl.BlockSpec((rows, d), lambda i, j: (0, 0)),
                  pl.BlockSpec((1, d, tn), lambda i, j: (i, 0, j)),
                  pl.BlockSpec((1, 1, tn), lambda i, j: (i, 0, j))],
        out_specs=pl.BlockSpec((1, rows, tn), lambda i, j: (i, 0, j)),
        out_shape=jax.ShapeDtypeStruct((depth, rows, n), F32),
        compiler_params=_cparams(("arbitrary", "arbitrary")),
        name="ada_table",
    )(cc, ada_w, ada_b.reshape(depth, 1, n))


class _Geom:
    def __init__(self, B, L, C):
        self.B, self.L, self.C = B, L, C
        self.n_lat = B * L
        self.n_ctx = B * C
        self.rows = self.n_lat + self.n_ctx
        tm = 1024
        while (L % tm) or (self.n_ctx % tm):
            tm //= 2
        self.tm = tm

    def seg(self, i, tm):
        r0 = i * tm
        return jnp.where(r0 < self.n_lat, r0 // self.L, self.B)


def _mm_kernel(*refs, n_pro, n_epi, prologue, epilogue):
    x_ref = refs[0]
    pro_refs = refs[1:1 + n_pro]
    w_ref = refs[1 + n_pro]
    epi_refs = refs[2 + n_pro:2 + n_pro + n_epi]
    o_ref = refs[2 + n_pro + n_epi]
    j = pl.program_id(1)
    if prologue is not None:
        h_ref = refs[3 + n_pro + n_epi]

        @pl.when(j == 0)
        def _():
            h_ref[...] = prologue(x_ref, pro_refs).astype(BF16)

        lhs = h_ref[...]
    else:
        lhs = x_ref[...]
    acc = _dot(lhs, w_ref[...])
    epilogue(o_ref, acc, epi_refs, j)


def _mm_call(name, x, w, *, n_rows, tm, tn, prologue, pro_inputs, epilogue, epi_inputs,
             out_dtype):
    K, N = w.shape
    in_specs = [pl.BlockSpec((tm, K), lambda i, j: (i, 0))]
    args = [x]
    for a, spec in pro_inputs:
        in_specs.append(spec)
        args.append(a)
    in_specs.append(pl.BlockSpec((K, tn), lambda i, j: (0, j)))
    args.append(w)
    for a, spec in epi_inputs:
        in_specs.append(spec)
        args.append(a)
    scratch = [pltpu.VMEM((tm, K), BF16)] if prologue is not None else []
    kern = functools.partial(_mm_kernel, n_pro=len(pro_inputs), n_epi=len(epi_inputs),
                             prologue=prologue, epilogue=epilogue)
    return pl.pallas_call(
        kern, grid=(n_rows // tm, N // tn), in_specs=in_specs,
        out_specs=pl.BlockSpec((tm, tn), lambda i, j: (i, j)),
        out_shape=jax.ShapeDtypeStruct((n_rows, N), out_dtype),
        scratch_shapes=scratch,
        compiler_params=_cparams(("arbitrary", "arbitrary")),
        name=name,
    )(*args)


def _mod_full_spec(geom, tm):
    return pl.BlockSpec((1, MOD_ROWS, D_MODEL), lambda i, j: (geom.seg(i, tm), 0, 0))


def _mod_tile_spec(geom, tm, tn):
    return pl.BlockSpec((1, MOD_ROWS, tn), lambda i, j: (geom.seg(i, tm), 0, j))


def _normmod_prologue(shift_row, scale_row):
    def prologue(x_ref, pro_refs):
        mod_ref, g_ref = pro_refs
        m = mod_ref[0]
        return _rms_mod(x_ref[...], g_ref[...], m[shift_row:shift_row + 1],
                        m[scale_row:scale_row + 1])
    return prologue


def _rope_epilogue(n_rope_tiles, tn):
    def epilogue(o_ref, acc, epi_refs, j):
        cos_ref, sin_ref = epi_refs

        @pl.when(j < n_rope_tiles)
        def _():
            cos = cos_ref[...]
            sin = sin_ref[...]
            lane = lax.broadcasted_iota(jnp.int32, cos.shape, 1)
            low = (lane & 16) == 0
            for s in range(tn // LANES):
                xs = acc[:, s * LANES:(s + 1) * LANES]
                partner = jnp.where(low, pltpu.roll(xs, LANES - 16, 1), pltpu.roll(xs, 16, 1))
                o_ref[:, s * LANES:(s + 1) * LANES] = (xs * cos + partner * sin).astype(o_ref.dtype)

        @pl.when(j >= n_rope_tiles)
        def _():
            o_ref[...] = acc.astype(o_ref.dtype)
    return epilogue


def _gated_residual_epilogue(gate_row):
    def epilogue(o_ref, acc, epi_refs, j):
        x_ref, mod_ref = epi_refs
        gate = mod_ref[0][gate_row:gate_row + 1]
        o_ref[...] = x_ref[...] + gate * acc
    return epilogue


def _gated_residual_norm_epilogue(gate_row):
    def epilogue(o_ref, acc, epi_refs, j):
        x_ref, mod_ref, g_ref = epi_refs
        gate = mod_ref[0][gate_row:gate_row + 1]
        x = x_ref[...] + gate * acc
        o_ref[...] = x * lax.rsqrt(jnp.mean(x * x, axis=-1, keepdims=True) + NORM_EPS) * g_ref[...]
    return epilogue


def _relu2_epilogue(o_ref, acc, epi_refs, j):
    a = jnp.maximum(acc, 0.0)
    o_ref[...] = (a * a).astype(o_ref.dtype)


def _rwkv_feat_kernel(*refs, tm, seg_lat, seg_ctx, n_lat, has_vres):
    (x_ref, xp_ref, xn_ref, mod_ref, ng_ref, mix_ref, wrkv_ref, w1_ref, w2f_ref, w2b_ref,
     g1_ref, g2f_ref, g2b_ref, a1_ref, a2_ref, vec_ref) = refs[:16]
    pos = 16
    if has_vres:
        v1_ref, v2_ref, vfirst_ref = refs[pos:pos + 3]
        pos += 3
    (r_o, k_o, v_o, kk_o, a_o, lw_o, g_o) = refs[pos:pos + 7]

    i = pl.program_id(0)
    m = mod_ref[0]
    shift, scale = m[0:1], m[1:2]
    g = ng_ref[...]
    h = _rms_mod(x_ref[...], g, shift, scale)
    hp = _rms_mod(xp_ref[...], g, shift, scale)[7:8]
    hn = _rms_mod(xn_ref[...], g, shift, scale)[0:1]

    row = lax.broadcasted_iota(jnp.int32, (tm, 1), 0)
    rid = row + i * tm
    in_lat = rid < n_lat
    seg_pos = jnp.where(in_lat, rid % seg_lat, (rid - n_lat) % seg_ctx)
    seg_len = jnp.where(in_lat, seg_lat, seg_ctx)
    prev = pltpu.roll(h, 1, 0)
    prev = jnp.where(row == 0, hp, prev)
    prev = jnp.where(seg_pos == 0, 0.0, prev)
    nxt = pltpu.roll(h, tm - 1, 0)
    nxt = jnp.where(row == tm - 1, hn, nxt)
    nxt = jnp.where(seg_pos == seg_len - 1, 0.0, nxt)
    xx = 0.5 * (prev + nxt) - h

    def mixed(mi):
        return (h + xx * mix_ref[mi:mi + 1]).astype(BF16)

    vec = vec_ref[...]
    ones_bd = _head_sum_mat()
    ones_bd2 = jnp.concatenate([ones_bd, ones_bd], axis=0)

    def store(o_ref, val):
        for p in range(N_SLAB):
            o_ref[p] = val[:, p * LANES:(p + 1) * LANES]

    xr = mixed(0)
    store(r_o, _dot(xr, wrkv_ref[0]))

    xw = mixed(1)
    tw = jnp.tanh(_dot(xw, w1_ref[...])).astype(BF16)
    neg_e = -math.exp(-0.5)
    store(lw_o.at[0], neg_e * jax.nn.sigmoid(vec[0:1] + _dot(tw, w2f_ref[...])))
    store(lw_o.at[1], neg_e * jax.nn.sigmoid(vec[1:2] + _dot(tw, w2b_ref[...])))

    xg = mixed(5)
    sg = jax.nn.sigmoid(_dot(xg, g1_ref[...])).astype(BF16)
    store(g_o.at[0], _dot(sg, g2f_ref[...]))
    store(g_o.at[1], _dot(sg, g2b_ref[...]))

    xa = mixed(4)
    a = jax.nn.sigmoid(vec[2:3] + _dot(_dot(xa, a1_ref[...]).astype(BF16), a2_ref[...]))
    store(a_o, a)

    xv = mixed(3)
    v = _dot(xv, wrkv_ref[2])
    if has_vres:
        gate = jax.nn.sigmoid(vec[5:6] + _dot(_dot(xv, v1_ref[...]).astype(BF16), v2_ref[...]))
        for p in range(N_SLAB):
            sl = slice(p * LANES, (p + 1) * LANES)
            vp = v[:, sl]
            v_o[p] = vp + (vfirst_ref[p] - vp) * gate[:, sl]
    else:
        store(v_o, v)

    xk = mixed(2)
    k = _dot(xk, wrkv_ref[1])
    k_scale = 1.0 + (a - 1.0) * vec[4:5]
    kraw = k * vec[3:4]
    for p in range(N_SLAB):
        sl = slice(p * LANES, (p + 1) * LANES)
        kr = kraw[:, sl]
        ss = _head_sum(kr * kr, ones_bd2)
        kk_o[p] = kr * lax.rsqrt(jnp.maximum(ss, 1e-24))
        k_o[p] = k[:, sl] * k_scale[:, sl]


def _rwkv_features(geom, x, mod, wts, vfirst):
    tm = 256
    R = geom.rows
    has_vres = vfirst is not None
    nblk8 = R // 8
    pm = functools.partial
    in_specs = [
        pl.BlockSpec((tm, D_MODEL), lambda i: (i, 0)),
        pl.BlockSpec((8, D_MODEL), lambda i: (jnp.maximum(i * (tm // 8) - 1, 0), 0)),
        pl.BlockSpec((8, D_MODEL), lambda i: (jnp.minimum((i + 1) * (tm // 8), nblk8 - 1), 0)),
        pl.BlockSpec((1, MOD_ROWS, D_MODEL), lambda i: (geom.seg(i, tm), 0, 0)),
    ]
    args = [x, x, x, mod]
    names = ["norm_g", "mix", "wrkv", "w1", "w2f", "w2b", "g1", "g2f", "g2b", "a1", "a2", "vec"]
    if has_vres:
        names += ["v1", "v2"]
    for nme in names:
        a = wts[nme]
        in_specs.append(_const_spec(a.shape))
        args.append(a)
    pair_spec = pl.BlockSpec((N_SLAB, tm, LANES), lambda i: (0, i, 0))
    dir_spec = pl.BlockSpec((2, N_SLAB, tm, LANES), lambda i: (0, 0, i, 0))
    if has_vres:
        in_specs.append(pair_spec)
        args.append(vfirst)
    pair_shape = jax.ShapeDtypeStruct((N_SLAB, R, LANES), F32)
    dir_shape = jax.ShapeDtypeStruct((2, N_SLAB, R, LANES), F32)
    kern = pm(_rwkv_feat_kernel, tm=tm, seg_lat=geom.L, seg_ctx=geom.C, n_lat=geom.n_lat,
              has_vres=has_vres)
    return pl.pallas_call(
        kern, grid=(R // tm,), in_specs=in_specs,
        out_specs=[pair_spec] * 5 + [dir_spec] * 2, out_shape=[pair_shape] * 5 + [dir_shape] * 2,
        compiler_params=_cparams(("arbitrary",)),
        name="rwkv_features",
    )(*args)


def _bd_rows(x, lane_lo):
    return jnp.concatenate([jnp.where(lane_lo, x, 0.0), jnp.where(lane_lo, 0.0, x)], axis=0)


def _wkv_kernel(r_ref, k_ref, v_ref, kk_ref, a_ref, lw_ref, o_ref, st_ref, *, n_chunk):
    d = pl.program_id(0)
    j = pl.program_id(2)
    c = CHUNK

    @pl.when(j == 0)
    def _():
        st_ref[...] = jnp.zeros_like(st_ref)

    sign = 1 - 2 * d
    t_i = lax.broadcasted_iota(jnp.int32, (c, LANES), 0)
    lane = lax.broadcasted_iota(jnp.int32, (c, LANES), 1)
    s_i = lane % c
    lane_lo = lane < c
    u = (t_i - s_i) * sign
    strict = u > 0
    incl = u >= 0
    eye = u == 0
    blk = [(t_i // n) == (s_i // n) for n in (8, 16, 32)]
    tc = lax.broadcasted_iota(jnp.int32, (c, 3 * c), 0)
    sc = lax.broadcasted_iota(jnp.int32, (c, 3 * c), 1) % c
    mcum = (((tc - sc) * sign) >= 0).astype(BF16)
    last_row = jnp.where(d == 0, c - 1, 0)
    row1 = lax.broadcasted_iota(jnp.int32, (c, 1), 0)
    is_last = row1 == last_row
    bd_avg = _head_sum_mat() * (1.0 / HEAD_DIM)
    bd_avg2 = jnp.concatenate([bd_avg, bd_avg], axis=0)
    kr = lax.broadcasted_iota(jnp.int32, (LANES, LANES), 0)
    kc = lax.broadcasted_iota(jnp.int32, (LANES, LANES), 1)
    same_head = (kr // HEAD_DIM) == (kc // HEAD_DIM)
    diag = kr == kc
    eye_f = jnp.where(eye, 1.0, 0.0)

    def bf(x):
        return x.astype(BF16)

    def pmul(x_pair, y_pair):
        return _dot(bf(x_pair), bf(_bd_rows(y_pair, lane_lo)))

    def chunk_body(ci, carry):
        c_eff = ci + d * (n_chunk - 1 - 2 * ci)
        ds = pl.ds(pl.multiple_of(c_eff * c, c), c)
        for p in range(N_SLAB):
            lw = lw_ref[0, p, ds, :]
            r = r_ref[p, ds, :]
            k = k_ref[p, ds, :]
            v = v_ref[p, ds, :]
            kk = kk_ref[p, ds, :]
            a = a_ref[p, ds, :]
            hi, mid, lo = _split3(lw)
            cum = _dot(mcum, jnp.concatenate([hi, mid, lo], axis=0))
            e_pos = jnp.exp(cum)
            e_exc = jnp.exp(cum - lw)
            e_neg = jnp.exp(-cum)
            p_end = jnp.sum(jnp.where(is_last, e_pos, 0.0), axis=0, keepdims=True)
            r_t = r * e_pos
            a_t = -kk * e_exc
            b_t = kk * a * e_neg
            k_t = k * e_neg
            lhs1 = bf(jnp.concatenate([a_t, r_t], axis=0))
            rhs1 = bf(jnp.concatenate([_bd_rows(b_t, lane_lo), _bd_rows(k_t, lane_lo)], axis=0))
            gram = _dot_nt(lhs1, rhs1)
            ab = gram[:c, :LANES]
            ak = gram[:c, LANES:]
            rb = gram[c:, :LANES]
            rk = gram[c:, LANES:]
            l_ab = jnp.where(strict, ab, 0.0)
            l_ak = jnp.where(strict, ak, 0.0)
            p_rb = jnp.where(incl, rb, 0.0)
            p_rk = jnp.where(incl, rk, 0.0)

            n0 = jnp.where(blk[0], l_ab, 0.0)
            p1 = pmul(n0, n0)
            t0 = eye_f + n0
            both = pmul(jnp.concatenate([t0, p1], axis=0), p1)
            t1 = t0 + both[:c]
            p2 = both[c:]
            tinv = t1 + pmul(t1, p2)
            prev_blk = blk[0]
            for nb in (blk[1], blk[2], None):
                cur = strict if nb is None else (strict & nb)
                off = jnp.where(cur & jnp.logical_not(prev_blk), l_ab, 0.0)
                tinv = tinv + pmul(tinv, pmul(off, tinv))
                prev_blk = nb

            st = st_ref[p]
            st_b = bf(st)
            v_bd = bf(_bd_rows(v, lane_lo))
            rhs_u = _dot(bf(jnp.concatenate([a_t, l_ak], axis=1)),
                         jnp.concatenate([st_b, v_bd], axis=0))
            u_p = pmul(tinv, rhs_u)
            u_bd = bf(_bd_rows(u_p, lane_lo))
            o = _dot(bf(jnp.concatenate([r_t, p_rb, p_rk], axis=1)),
                     jnp.concatenate([st_b, u_bd, v_bd], axis=0))

            y = jnp.concatenate([b_t * p_end, k_t * p_end], axis=0)
            xcat = jnp.concatenate([u_p, v], axis=0)
            upd = _dot(bf(y.T), bf(xcat))
            p_col = jnp.sum(jnp.where(diag, p_end, 0.0), axis=1, keepdims=True)
            st_ref[p] = st * p_col + jnp.where(same_head, upd, 0.0)

            ohi, olo = _split2(o)
            mu = _dot(jnp.concatenate([ohi, olo], axis=1), bf(bd_avg2))
            dlt = o - mu
            dhi, dlo = _split2(dlt * dlt)
            var = _dot(jnp.concatenate([dhi, dlo], axis=1), bf(bd_avg2))
            o_ref[0, p, ds, :] = dlt * lax.rsqrt(var + GN_EPS)
        return carry

    lax.fori_loop(0, n_chunk, chunk_body, 0)


def _wkv_scan(geom, r, k, v, kk, a, lw):
    B, L, C = geom.B, geom.L, geom.C
    blk = SCAN_BLOCK
    n_lat_blk = L // blk
    n_ctx_blk = C // blk
    n_steps = n_lat_blk + n_ctx_blk
    lat_blocks = geom.n_lat // blk

    def row_block(d, b, j):
        in_ctx = j < n_ctx_blk
        jc = jnp.where(d == 0, j, n_ctx_blk - 1 - j)
        jl = jnp.where(d == 0, j - n_ctx_blk, n_steps - 1 - j)
        return jnp.where(in_ctx, lat_blocks + b * n_ctx_blk + jc, b * n_lat_blk + jl)

    spec = pl.BlockSpec((N_SLAB, blk, LANES), lambda d, b, j: (0, row_block(d, b, j), 0))
    dir_spec = pl.BlockSpec((1, N_SLAB, blk, LANES), lambda d, b, j: (d, 0, row_block(d, b, j), 0))
    kern = functools.partial(_wkv_kernel, n_chunk=blk // CHUNK)
    return pl.pallas_call(
        kern, grid=(2, B, n_steps),
        in_specs=[spec, spec, spec, spec, spec, dir_spec],
        out_specs=dir_spec,
        out_shape=jax.ShapeDtypeStruct((2, N_SLAB, geom.rows, LANES), F32),
        scratch_shapes=[pltpu.VMEM((N_SLAB, LANES, LANES), F32)],
        compiler_params=_cparams(("arbitrary", "arbitrary", "arbitrary")),
        name="wkv_scan",
    )(r, k, v, kk, a, lw)


def _rwkv_out_kernel(on_ref, r_ref, k_ref, v_ref, g_ref, vec_ref, w_ref, x_ref, mod_ref,
                     o_ref, h_ref):
    j = pl.program_id(1)

    @pl.when(j == 0)
    def _():
        ones_bd = _head_sum_mat()
        ones_bd2 = jnp.concatenate([ones_bd, ones_bd], axis=0)
        vec = vec_ref[...]
        for p in range(N_SLAB):
            sl = slice(p * LANES, (p + 1) * LANES)
            rk, lg, lb = vec[0:1, sl], vec[1:2, sl], vec[2:3, sl]
            bonus = _head_sum(r_ref[p] * k_ref[p] * rk, ones_bd2) * v_ref[p]
            y = ((on_ref[0, p] * lg + lb + bonus) * g_ref[0, p]
                 + (on_ref[1, p] * lg + lb + bonus) * g_ref[1, p])
            h_ref[:, sl] = y.astype(BF16)

    acc = _dot(h_ref[...], w_ref[...])
    o_ref[...] = x_ref[...] + mod_ref[0][2:3] * acc


def _rwkv_out(geom, on, r, k, v, g, vec, w_o, x, mod):
    tm, tn = 256, 512
    R = geom.rows
    pair_spec = pl.BlockSpec((N_SLAB, tm, LANES), lambda i, j: (0, i, 0))
    dir_spec = pl.BlockSpec((2, N_SLAB, tm, LANES), lambda i, j: (0, 0, i, 0))
    return pl.pallas_call(
        _rwkv_out_kernel, grid=(R // tm, D_MODEL // tn),
        in_specs=[dir_spec, pair_spec, pair_spec, pair_spec, dir_spec,
                  _const_spec(vec.shape),
                  pl.BlockSpec((D_MODEL, tn), lambda i, j: (0, j)),
                  pl.BlockSpec((tm, tn), lambda i, j: (i, j)),
                  pl.BlockSpec((1, MOD_ROWS, tn), lambda i, j: (geom.seg(i, tm), 0, j))],
        out_specs=pl.BlockSpec((tm, tn), lambda i, j: (i, j)),
        out_shape=jax.ShapeDtypeStruct((R, D_MODEL), F32),
        scratch_shapes=[pltpu.VMEM((tm, D_MODEL), BF16)],
        compiler_params=_cparams(("arbitrary", "arbitrary")),
        name="rwkv_out",
    )(on, r, k, v, g, vec, w_o, x, mod)


def _attn_kernel(q_ref, kc_ref, vc_ref, kl_ref, vl_ref, lam_ref, sg_ref, o_ref, m_ref, l_ref,
                 acc_ref, *, tq, tk, n_lat_chunks, n_lat_q, lambda_init):
    qi = pl.program_id(2)
    lane = lax.broadcasted_iota(jnp.int32, (tq, LANES), 1)
    q = q_ref[...]
    scale = HEAD_DIM ** -0.5
    zero = jnp.zeros_like(q)
    qs = (jnp.where(lane < HEAD_DIM, q, zero) * scale, jnp.where(lane < HEAD_DIM, zero, q) * scale)

    m_ref[...] = jnp.full_like(m_ref, -jnp.inf)
    l_ref[...] = jnp.zeros_like(l_ref)
    acc_ref[...] = jnp.zeros_like(acc_ref)

    def step(k, v):
        for s in range(2):
            sc = _dot_nt(qs[s], k)
            m_old = m_ref[s]
            m_new = jnp.maximum(m_old, jnp.max(sc, axis=-1, keepdims=True))
            alpha = jnp.exp(m_old - m_new)
            pr = jnp.exp(sc - m_new)
            l_ref[s] = alpha * l_ref[s] + jnp.sum(pr, axis=-1, keepdims=True)
            acc_ref[s] = alpha * acc_ref[s] + _dot(pr.astype(BF16), v)
            m_ref[s] = m_new

    step(kc_ref[...], vc_ref[...])

    def body(ci, carry):
        ds = pl.ds(pl.multiple_of(ci * tk, tk), tk)
        step(kl_ref[ds, :], vl_ref[ds, :])
        return carry

    lax.fori_loop(0, jnp.where(qi < n_lat_q, n_lat_chunks, 0), body, 0)

    lv = lam_ref[...]
    lam = (jnp.exp(jnp.sum(lv[0:1] * lv[1:2], axis=-1, keepdims=True))
           - jnp.exp(jnp.sum(lv[2:3] * lv[3:4], axis=-1, keepdims=True)) + lambda_init)
    o = acc_ref[0] / l_ref[0] - lam * (acc_ref[1] / l_ref[1])
    o = o * lax.rsqrt(jnp.mean(o * o, axis=-1, keepdims=True) + SUBLN_EPS) * sg_ref[0:1]
    o_ref[...] = (o * (1.0 - lambda_init)).astype(o_ref.dtype)


def _attention(geom, qkv, lam_vec, subln_g, lambda_init, *, need_ctx):
    B, L, C = geom.B, geom.L, geom.C
    n_heads = D_MODEL // LANES
    tq = min(256, C)
    tk = min(512, L)
    n_lat_q = L // tq
    n_ctx_q = C // tq if need_ctx else 0
    ctx_q_blk = geom.n_lat // tq
    ctx_row_blk = geom.n_lat // C

    def q_blk(b, qi):
        return jnp.where(qi < n_lat_q, b * n_lat_q + qi, ctx_q_blk + b * (C // tq) + qi - n_lat_q)

    in_specs = [pl.BlockSpec((tq, LANES), lambda b, h, qi: (q_blk(b, qi), h)),
                pl.BlockSpec((C, LANES), lambda b, h, qi: (ctx_row_blk + b, n_heads + h)),
                pl.BlockSpec((C, LANES), lambda b, h, qi: (ctx_row_blk + b, 2 * n_heads + h)),
                pl.BlockSpec((L, LANES), lambda b, h, qi: (b, n_heads + h)),
                pl.BlockSpec((L, LANES), lambda b, h, qi: (b, 2 * n_heads + h)),
                _const_spec(lam_vec.shape), _const_spec(subln_g.shape)]
    kern = functools.partial(_attn_kernel, tq=tq, tk=tk, n_lat_chunks=L // tk, n_lat_q=n_lat_q,
                             lambda_init=lambda_init)
    return pl.pallas_call(
        kern, grid=(B, n_heads, n_lat_q + n_ctx_q), in_specs=in_specs,
        out_specs=pl.BlockSpec((tq, LANES), lambda b, h, qi: (q_blk(b, qi), h)),
        out_shape=jax.ShapeDtypeStruct((geom.rows if need_ctx else geom.n_lat, D_MODEL), BF16),
        scratch_shapes=[pltpu.VMEM((2, tq, 1), F32), pltpu.VMEM((2, tq, 1), F32),
                        pltpu.VMEM((2, tq, LANES), F32)],
        compiler_params=_cparams(("arbitrary", "arbitrary", "arbitrary")),
        name="diff_attention",
    )(qkv, qkv, qkv, qkv, qkv, lam_vec, subln_g)


def _rope_tables(geom):
    L = geom.L
    t = jnp.arange(L, dtype=jnp.int32)
    inv = ROPE_THETA ** (-jnp.arange(ROPE_PAIRS, dtype=F32) / ROPE_PAIRS)
    lane = jnp.arange(LANES, dtype=jnp.int32) % HEAD_DIM
    freq = inv[lane % ROPE_PAIRS]
    pos = jnp.where((lane < HEAD_DIM // 2)[None, :], (t // GRID_W)[:, None], (t % GRID_W)[:, None])
    ang = pos.astype(F32) * freq[None, :]
    sign = jnp.where((lane % 32) < 16, -1.0, 1.0)
    cos_l = jnp.cos(ang)
    sin_l = jnp.sin(ang) * sign[None, :]
    cos = jnp.concatenate([jnp.tile(cos_l, (geom.B, 1)), jnp.ones((geom.n_ctx, LANES), F32)], axis=0)
    sin = jnp.concatenate([jnp.tile(sin_l, (geom.B, 1)), jnp.zeros((geom.n_ctx, LANES), F32)], axis=0)
    return cos, sin


def _pad_cols(w, n):
    return jnp.pad(w, ((0, 0), (0, n - w.shape[1])))


def _pad_rows(w, n, offset=0):
    return jnp.pad(w, ((offset, n - offset - w.shape[0]), (0, 0)))


def _rwkv_weights(j, norm_g_row, rw_mix, rw_w_rkv, rw_w0, rw_w1, rw_w2, rw_a0, rw_a1, rw_a2, rw_g1,
                  rw_g2, rw_kk, rw_ka, rw_v0, rw_v1, rw_v2):
    lora_w = rw_w1.shape[-1]
    lora_g = rw_g1.shape[-1]
    g_pad = 3 * LANES
    wts = {
        "norm_g": norm_g_row.reshape(1, D_MODEL),
        "mix": jnp.pad(rw_mix[j], ((0, 2), (0, 0))),
        "wrkv": rw_w_rkv[j].astype(BF16),
        "w1": jnp.concatenate([rw_w1[j, 0], rw_w1[j, 1]], axis=1).astype(BF16),
        "w2f": _pad_rows(rw_w2[j, 0], LANES).astype(BF16),
        "w2b": _pad_rows(rw_w2[j, 1], LANES, lora_w).astype(BF16),
        "g1": _pad_cols(jnp.concatenate([rw_g1[j, 0], rw_g1[j, 1]], axis=1), g_pad).astype(BF16),
        "g2f": _pad_rows(rw_g2[j, 0], g_pad).astype(BF16),
        "g2b": _pad_rows(rw_g2[j, 1], g_pad, lora_g).astype(BF16),
        "a1": _pad_cols(rw_a1[j], LANES).astype(BF16),
        "a2": _pad_rows(rw_a2[j], LANES).astype(BF16),
    }
    v0 = rw_v0[j - 1] if j > 0 else jnp.zeros((D_MODEL,), F32)
    wts["vec"] = jnp.stack([rw_w0[j, 0], rw_w0[j, 1], rw_a0[j], rw_kk[j], rw_ka[j], v0,
                            jnp.zeros((D_MODEL,), F32), jnp.zeros((D_MODEL,), F32)], axis=0)
    if j > 0:
        wts["v1"] = _pad_cols(rw_v1[j - 1], LANES).astype(BF16)
        wts["v2"] = _pad_rows(rw_v2[j - 1], LANES).astype(BF16)
    return wts


def kernel(x, c, ctx, c_ctx, ada_w, ada_b, norm_g, final_g, rw_mix, rw_w_rkv, rw_w0, rw_w1, rw_w2, rw_a0, rw_a1, rw_a2, rw_g1, rw_g2, rw_kk, rw_ka, rw_rk, rw_ln_g, rw_ln_b, rw_w_o, rw_v0, rw_v1, rw_v2, da_w_qkv, da_w_o, da_lq1, da_lk1, da_lq2, da_lk2, da_subln_g, mlp_w1, mlp_w2):
    B, L, D = x.shape
    C = ctx.shape[1]
    depth = ada_w.shape[0]
    geom = _Geom(B, L, C)
    tm = geom.tm

    n_seg = 16
    cc = jnp.concatenate([c, c_ctx[None, :], jnp.zeros((n_seg - B - 1, D), F32)], axis=0)
    mod_all = _ada_table(cc, ada_w, ada_b).reshape(depth, n_seg, 6, D)
    mod_all = jnp.pad(mod_all, ((0, 0), (0, 0), (0, MOD_ROWS - 6), (0, 0)))

    xs = jnp.concatenate([x.reshape(B * L, D), ctx.reshape(B * C, D)], axis=0)
    rope_cos, rope_sin = _rope_tables(geom)
    vfirst = None

    for i in range(depth):
        last = i == depth - 1
        mod = mod_all[i]
        j = i // 2
        ng0 = norm_g[i, 0].reshape(1, D)
        ng1 = norm_g[i, 1].reshape(1, D)
        n_rows = geom.n_lat if last else geom.rows
        tn = 512
        xtile = pl.BlockSpec((tm, tn), lambda ti, tj: (ti, tj))
        if i % 2 == 0:
            wts = _rwkv_weights(j, norm_g[i, 0], rw_mix, rw_w_rkv, rw_w0, rw_w1, rw_w2, rw_a0, rw_a1,
                                rw_a2, rw_g1, rw_g2, rw_kk, rw_ka, rw_v0, rw_v1, rw_v2)
            r, k, v, kk, a, lw, g = _rwkv_features(geom, xs, mod, wts, vfirst)
            if j == 0:
                vfirst = v
            on = _wkv_scan(geom, r, k, v, kk, a, lw)
            vec = jnp.stack([rw_rk[j].reshape(D), rw_ln_g[j], rw_ln_b[j]]
                            + [jnp.zeros((D,), F32)] * 5, axis=0)
            xs = _rwkv_out(geom, on, r, k, v, g, vec, rw_w_o[j].astype(BF16), xs, mod)
        else:
            lambda_init = 0.8 - 0.6 * math.exp(-0.3 * i)
            rope_spec = pl.BlockSpec((tm, LANES), lambda ti, tj: (ti, 0))
            qkv = _mm_call("attn_qkv", xs, da_w_qkv[j].astype(BF16), n_rows=geom.rows, tm=tm, tn=tn,
                           prologue=_normmod_prologue(0, 1),
                           pro_inputs=[(mod, _mod_full_spec(geom, tm)), (ng0, _const_spec(ng0.shape))],
                           epilogue=_rope_epilogue(2 * D // tn, tn),
                           epi_inputs=[(rope_cos, rope_spec), (rope_sin, rope_spec)],
                           out_dtype=BF16)
            lam_vec = jnp.pad(jnp.stack([da_lq1[j], da_lk1[j], da_lq2[j], da_lk2[j]], axis=0),
                              ((0, 4), (0, LANES - HEAD_DIM)))
            sg = da_subln_g[j].reshape(1, LANES)
            o_att = _attention(geom, qkv, lam_vec, sg, lambda_init, need_ctx=not last)
            xs = _mm_call("attn_out", o_att, da_w_o[j].astype(BF16), n_rows=n_rows, tm=tm, tn=tn,
                          prologue=None, pro_inputs=[], epilogue=_gated_residual_epilogue(2),
                          epi_inputs=[(xs, xtile), (mod, _mod_tile_spec(geom, tm, tn))],
                          out_dtype=F32)
        hmid = _mm_call("mlp_up", xs, mlp_w1[i].astype(BF16), n_rows=n_rows, tm=tm, tn=tn,
                        prologue=_normmod_prologue(3, 4),
                        pro_inputs=[(mod, _mod_full_spec(geom, tm)), (ng1, _const_spec(ng1.shape))],
                        epilogue=_relu2_epilogue, epi_inputs=[], out_dtype=BF16)
        if last:
            tmf = min(tm, 512)
            fg = final_g.reshape(1, D)
            xs = _mm_call("mlp_down_final", hmid, mlp_w2[i].astype(BF16), n_rows=n_rows, tm=tmf, tn=D,
                          prologue=None, pro_inputs=[], epilogue=_gated_residual_norm_epilogue(5),
                          epi_inputs=[(xs, pl.BlockSpec((tmf, D), lambda ti, tj: (ti, 0))),
                                      (mod, _mod_tile_spec(geom, tmf, D)),
                                      (fg, _const_spec(fg.shape))],
                          out_dtype=F32)
        else:
            xs = _mm_call("mlp_down", hmid, mlp_w2[i].astype(BF16), n_rows=n_rows, tm=tm, tn=tn,
                          prologue=None, pro_inputs=[], epilogue=_gated_residual_epilogue(5),
                          epi_inputs=[(xs, xtile), (mod, _mod_tile_spec(geom, tm, tn))],
                          out_dtype=F32)
    return xs[:B * L].reshape(B, L, D)
```

```python
import functools
import math

import jax
import jax.numpy as jnp
from jax import lax
from jax.experimental import pallas as pl
from jax.experimental.pallas import tpu as pltpu

D_MODEL = 1024
HEAD_DIM = 64
LANES = 128
N_SLAB = D_MODEL // LANES
D_FF = 4 * D_MODEL
GRID_W = 64
ROPE_THETA = 10000.0
ROPE_PAIRS = HEAD_DIM // 4
NORM_EPS = 1e-6
SUBLN_EPS = 1e-5
GN_EPS = 64e-5
CHUNK = 64
SCAN_BLOCK = 256
ATTN_TQ = 512
ATTN_TK = 512
MOD_ROWS = 8
VMEM_LIMIT = 56 * 1024 * 1024

F32 = jnp.float32
BF16 = jnp.bfloat16


def _cparams(sem):
    return pltpu.CompilerParams(dimension_semantics=sem, vmem_limit_bytes=VMEM_LIMIT)


def _const_spec(shape):
    nd = len(shape)
    return pl.BlockSpec(shape, lambda *_: (0,) * nd, pipeline_mode=pl.Buffered(1))


def _dot(a, b):
    return jnp.dot(a, b, preferred_element_type=F32)


def _dot_nt(a, b):
    return lax.dot_general(a, b, (((1,), (1,)), ((), ())), preferred_element_type=F32)


def _split2(x):
    hi = x.astype(BF16)
    lo = (x - hi.astype(F32)).astype(BF16)
    return hi, lo


def _split3(x):
    hi = x.astype(BF16)
    r1 = x - hi.astype(F32)
    mid = r1.astype(BF16)
    lo = (r1 - mid.astype(F32)).astype(BF16)
    return hi, mid, lo


def _rms_mod(x, g, shift, scale):
    y = x * lax.rsqrt(jnp.mean(x * x, axis=-1, keepdims=True) + NORM_EPS) * g
    return y * (1.0 + scale) + shift


def _head_sum_mat():
    r = lax.broadcasted_iota(jnp.int32, (LANES, LANES), 0) // HEAD_DIM
    c = lax.broadcasted_iota(jnp.int32, (LANES, LANES), 1) // HEAD_DIM
    return (r == c).astype(BF16)


def _head_sum(x, ones_bd2):
    hi, lo = _split2(x)
    return _dot(jnp.concatenate([hi, lo], axis=1), ones_bd2)


def _ada_kernel(cc_ref, w_ref, b_ref, o_ref):
    cc = cc_ref[...]
    s = cc * jax.nn.sigmoid(cc)
    o_ref[0] = jnp.dot(s, w_ref[0], preferred_element_type=F32,
                       precision=lax.Precision.HIGHEST) + b_ref[0]


def _ada_table(cc, ada_w, ada_b):
    depth, d, n = ada_w.shape
    rows = cc.shape[0]
    tn = 1536
    return pl.pallas_call(
        _ada_kernel,
        grid=(depth, n // tn),
        in_specs=[pl.BlockSpec((rows, d), lambda i, j: (0, 0)),
                  pl.BlockSpec((1, d, tn), lambda i, j: (i, 0, j)),
                  pl.BlockSpec((1, 1, tn), lambda i, j: (i, 0, j))],
        out_specs=pl.BlockSpec((1, rows, tn), lambda i, j: (i, 0, j)),
        out_shape=jax.ShapeDtypeStruct((depth, rows, n), F32),
        compiler_params=_cparams(("arbitrary", "arbitrary")),
        name="ada_table",
    )(cc, ada_w, ada_b.reshape(depth, 1, n))


class _Geom:
    def __init__(self, B, L, C):
        self.B, self.L, self.C = B, L, C
        self.n_lat = B * L
        self.n_ctx = B * C
        self.rows = self.n_lat + self.n_ctx
        tm = 1024
        while (L % tm) or (self.n_ctx % tm):
            tm //= 2
        self.tm = tm

    def seg(self, i, tm):
        r0 = i * tm
        return jnp.where(r0 < self.n_lat, r0 // self.L, self.B)


def _mm_kernel(*refs, n_pro, n_epi, prologue, epilogue):
    x_ref = refs[0]
    pro_refs = refs[1:1 + n_pro]
    w_ref = refs[1 + n_pro]
    epi_refs = refs[2 + n_pro:2 + n_pro + n_epi]
    o_ref = refs[2 + n_pro + n_epi]
    j = pl.program_id(1)
    if prologue is not None:
        h_ref = refs[3 + n_pro + n_epi]

        @pl.when(j == 0)
        def _():
            h_ref[...] = prologue(x_ref, pro_refs).astype(BF16)

        lhs = h_ref[...]
    else:
        lhs = x_ref[...]
    acc = _dot(lhs, w_ref[...])
    epilogue(o_ref, acc, epi_refs, j)


def _mm_call(name, x, w, *, n_rows, tm, tn, prologue, pro_inputs, epilogue, epi_inputs,
             out_dtype):
    K, N = w.shape
    in_specs = [pl.BlockSpec((tm, K), lambda i, j: (i, 0))]
    args = [x]
    for a, spec in pro_inputs:
        in_specs.append(spec)
        args.append(a)
    in_specs.append(pl.BlockSpec((K, tn), lambda i, j: (0, j)))
    args.append(w)
    for a, spec in epi_inputs:
        in_specs.append(spec)
        args.append(a)
    scratch = [pltpu.VMEM((tm, K), BF16)] if prologue is not None else []
    kern = functools.partial(_mm_kernel, n_pro=len(pro_inputs), n_epi=len(epi_inputs),
                             prologue=prologue, epilogue=epilogue)
    return pl.pallas_call(
        kern, grid=(n_rows // tm, N // tn), in_specs=in_specs,
        out_specs=pl.BlockSpec((tm, tn), lambda i, j: (i, j)),
        out_shape=jax.ShapeDtypeStruct((n_rows, N), out_dtype),
        scratch_shapes=scratch,
        compiler_params=_cparams(("arbitrary", "arbitrary")),
        name=name,
    )(*args)


def _mod_full_spec(geom, tm):
    return pl.BlockSpec((1, MOD_ROWS, D_MODEL), lambda i, j: (geom.seg(i, tm), 0, 0))


def _mod_tile_spec(geom, tm, tn):
    return pl.BlockSpec((1, MOD_ROWS, tn), lambda i, j: (geom.seg(i, tm), 0, j))


def _normmod_prologue(shift_row, scale_row):
    def prologue(x_ref, pro_refs):
        mod_ref, g_ref = pro_refs
        m = mod_ref[0]
        return _rms_mod(x_ref[...], g_ref[...], m[shift_row:shift_row + 1],
                        m[scale_row:scale_row + 1])
    return prologue


def _rope_epilogue(n_rope_tiles, tn):
    def epilogue(o_ref, acc, epi_refs, j):
        cos_ref, sin_ref = epi_refs

        @pl.when(j < n_rope_tiles)
        def _():
            cos = cos_ref[...]
            sin = sin_ref[...]
            lane = lax.broadcasted_iota(jnp.int32, cos.shape, 1)
            low = (lane & 16) == 0
            for s in range(tn // LANES):
                xs = acc[:, s * LANES:(s + 1) * LANES]
                partner = jnp.where(low, pltpu.roll(xs, LANES - 16, 1), pltpu.roll(xs, 16, 1))
                o_ref[:, s * LANES:(s + 1) * LANES] = (xs * cos + partner * sin).astype(o_ref.dtype)

        @pl.when(j >= n_rope_tiles)
        def _():
            o_ref[...] = acc.astype(o_ref.dtype)
    return epilogue


def _gated_residual_epilogue(gate_row):
    def epilogue(o_ref, acc, epi_refs, j):
        x_ref, mod_ref = epi_refs
        gate = mod_ref[0][gate_row:gate_row + 1]
        o_ref[...] = x_ref[...] + gate * acc
    return epilogue


def _gated_residual_norm_epilogue(gate_row):
    def epilogue(o_ref, acc, epi_refs, j):
        x_ref, mod_ref, g_ref = epi_refs
        gate = mod_ref[0][gate_row:gate_row + 1]
        x = x_ref[...] + gate * acc
        o_ref[...] = x * lax.rsqrt(jnp.mean(x * x, axis=-1, keepdims=True) + NORM_EPS) * g_ref[...]
    return epilogue


def _relu2_epilogue(o_ref, acc, epi_refs, j):
    a = jnp.maximum(acc, 0.0)
    o_ref[...] = (a * a).astype(o_ref.dtype)


def _rwkv_feat_kernel(*refs, tm, seg_lat, seg_ctx, n_lat, has_vres):
    (x_ref, xp_ref, xn_ref, mod_ref, ng_ref, mix_ref, wrkv_ref, w1_ref, w2f_ref, w2b_ref,
     g1_ref, g2f_ref, g2b_ref, a1_ref, a2_ref, vec_ref) = refs[:16]
    pos = 16
    if has_vres:
        v1_ref, v2_ref, vfirst_ref = refs[pos:pos + 3]
        pos += 3
    (r_o, k_o, v_o, kk_o, a_o, lw_o, g_o) = refs[pos:pos + 7]

    i = pl.program_id(0)
    m = mod_ref[0]
    shift, scale = m[0:1], m[1:2]
    g = ng_ref[...]
    h = _rms_mod(x_ref[...], g, shift, scale)
    hp = _rms_mod(xp_ref[...], g, shift, scale)[7:8]
    hn = _rms_mod(xn_ref[...], g, shift, scale)[0:1]

    row = lax.broadcasted_iota(jnp.int32, (tm, 1), 0)
    rid = row + i * tm
    in_lat = rid < n_lat
    seg_pos = jnp.where(in_lat, rid % seg_lat, (rid - n_lat) % seg_ctx)
    seg_len = jnp.where(in_lat, seg_lat, seg_ctx)
    prev = pltpu.roll(h, 1, 0)
    prev = jnp.where(row == 0, hp, prev)
    prev = jnp.where(seg_pos == 0, 0.0, prev)
    nxt = pltpu.roll(h, tm - 1, 0)
    nxt = jnp.where(row == tm - 1, hn, nxt)
    nxt = jnp.where(seg_pos == seg_len - 1, 0.0, nxt)
    xx = 0.5 * (prev + nxt) - h

    def mixed(mi):
        return (h + xx * mix_ref[mi:mi + 1]).astype(BF16)

    vec = vec_ref[...]
    ones_bd = _head_sum_mat()
    ones_bd2 = jnp.concatenate([ones_bd, ones_bd], axis=0)

    def store(o_ref, val):
        for p in range(N_SLAB):
            o_ref[p] = val[:, p * LANES:(p + 1) * LANES]

    xr = mixed(0)
    store(r_o, _dot(xr, wrkv_ref[0]))

    xw = mixed(1)
    tw = jnp.tanh(_dot(xw, w1_ref[...])).astype(BF16)
    neg_e = -math.exp(-0.5)
    store(lw_o.at[0], neg_e * jax.nn.sigmoid(vec[0:1] + _dot(tw, w2f_ref[...])))
    store(lw_o.at[1], neg_e * jax.nn.sigmoid(vec[1:2] + _dot(tw, w2b_ref[...])))

    xg = mixed(5)
    sg = jax.nn.sigmoid(_dot(xg, g1_ref[...])).astype(BF16)
    store(g_o.at[0], _dot(sg, g2f_ref[...]))
    store(g_o.at[1], _dot(sg, g2b_ref[...]))

    xa = mixed(4)
    a = jax.nn.sigmoid(vec[2:3] + _dot(_dot(xa, a1_ref[...]).astype(BF16), a2_ref[...]))
    store(a_o, a)

    xv = mixed(3)
    v = _dot(xv, wrkv_ref[2])
    if has_vres:
        gate = jax.nn.sigmoid(vec[5:6] + _dot(_dot(xv, v1_ref[...]).astype(BF16), v2_ref[...]))
        for p in range(N_SLAB):
            sl = slice(p * LANES, (p + 1) * LANES)
            vp = v[:, sl]
            v_o[p] = vp + (vfirst_ref[p] - vp) * gate[:, sl]
    else:
        store(v_o, v)

    xk = mixed(2)
    k = _dot(xk, wrkv_ref[1])
    k_scale = 1.0 + (a - 1.0) * vec[4:5]
    kraw = k * vec[3:4]
    for p in range(N_SLAB):
        sl = slice(p * LANES, (p + 1) * LANES)
        kr = kraw[:, sl]
        ss = _head_sum(kr * kr, ones_bd2)
        kk_o[p] = kr * lax.rsqrt(jnp.maximum(ss, 1e-24))
        k_o[p] = k[:, sl] * k_scale[:, sl]


def _rwkv_features(geom, x, mod, wts, vfirst):
    tm = 256
    R = geom.rows
    has_vres = vfirst is not None
    nblk8 = R // 8
    pm = functools.partial
    in_specs = [
        pl.BlockSpec((tm, D_MODEL), lambda i: (i, 0)),
        pl.BlockSpec((8, D_MODEL), lambda i: (jnp.maximum(i * (tm // 8) - 1, 0), 0)),
        pl.BlockSpec((8, D_MODEL), lambda i: (jnp.minimum((i + 1) * (tm // 8), nblk8 - 1), 0)),
        pl.BlockSpec((1, MOD_ROWS, D_MODEL), lambda i: (geom.seg(i, tm), 0, 0)),
    ]
    args = [x, x, x, mod]
    names = ["norm_g", "mix", "wrkv", "w1", "w2f", "w2b", "g1", "g2f", "g2b", "a1", "a2", "vec"]
    if has_vres:
        names += ["v1", "v2"]
    for nme in names:
        a = wts[nme]
        in_specs.append(_const_spec(a.shape))
        args.append(a)
    pair_spec = pl.BlockSpec((N_SLAB, tm, LANES), lambda i: (0, i, 0))
    dir_spec = pl.BlockSpec((2, N_SLAB, tm, LANES), lambda i: (0, 0, i, 0))
    if has_vres:
        in_specs.append(pair_spec)
        args.append(vfirst)
    pair_shape = jax.ShapeDtypeStruct((N_SLAB, R, LANES), F32)
    dir_shape = jax.ShapeDtypeStruct((2, N_SLAB, R, LANES), F32)
    kern = pm(_rwkv_feat_kernel, tm=tm, seg_lat=geom.L, seg_ctx=geom.C, n_lat=geom.n_lat,
              has_vres=has_vres)
    return pl.pallas_call(
        kern, grid=(R // tm,), in_specs=in_specs,
        out_specs=[pair_spec] * 5 + [dir_spec] * 2, out_shape=[pair_shape] * 5 + [dir_shape] * 2,
        compiler_params=_cparams(("arbitrary",)),
        name="rwkv_features",
    )(*args)


def _bd_rows(x, lane_lo):
    return jnp.concatenate([jnp.where(lane_lo, x, 0.0), jnp.where(lane_lo, 0.0, x)], axis=0)


def _wkv_kernel(r_ref, k_ref, v_ref, kk_ref, a_ref, lw_ref, o_ref, st_ref, *, n_chunk):
    d = pl.program_id(0)
    j = pl.program_id(2)
    c = CHUNK

    @pl.when(j == 0)
    def _():
        st_ref[...] = jnp.zeros_like(st_ref)

    sign = 1 - 2 * d
    t_i = lax.broadcasted_iota(jnp.int32, (c, LANES), 0)
    lane = lax.broadcasted_iota(jnp.int32, (c, LANES), 1)
    s_i = lane % c
    lane_lo = lane < c
    u = (t_i - s_i) * sign
    strict = u > 0
    incl = u >= 0
    eye = u == 0
    blk = [(t_i // n) == (s_i // n) for n in (8, 16, 32)]
    tc = lax.broadcasted_iota(jnp.int32, (c, 3 * c), 0)
    sc = lax.broadcasted_iota(jnp.int32, (c, 3 * c), 1) % c
    mcum = (((tc - sc) * sign) >= 0).astype(BF16)
    last_row = jnp.where(d == 0, c - 1, 0)
    row1 = lax.broadcasted_iota(jnp.int32, (c, 1), 0)
    is_last = row1 == last_row
    bd_avg = _head_sum_mat() * (1.0 / HEAD_DIM)
    bd_avg2 = jnp.concatenate([bd_avg, bd_avg], axis=0)
    kr = lax.broadcasted_iota(jnp.int32, (LANES, LANES), 0)
    kc = lax.broadcasted_iota(jnp.int32, (LANES, LANES), 1)
    same_head = (kr // HEAD_DIM) == (kc // HEAD_DIM)
    diag = kr == kc
    eye_f = jnp.where(eye, 1.0, 0.0)

    def bf(x):
        return x.astype(BF16)

    def pmul(x_pair, y_pair):
        return _dot(bf(x_pair), bf(_bd_rows(y_pair, lane_lo)))

    def chunk_body(ci, carry):
        c_eff = ci + d * (n_chunk - 1 - 2 * ci)
        ds = pl.ds(pl.multiple_of(c_eff * c, c), c)
        P = range(N_SLAB)

        def each(fn, *lists):
            return [fn(*args) for args in zip(*lists)]

        lw = [lw_ref[0, p, ds, :] for p in P]
        cum = each(lambda x: _dot(mcum, jnp.concatenate(_split3(x), axis=0)), lw)
        e_pos = each(jnp.exp, cum)
        e_neg = each(lambda x: jnp.exp(-x), cum)
        e_exc = each(lambda x, y: jnp.exp(x - y), cum, lw)
        p_end = each(lambda e: jnp.sum(jnp.where(is_last, e, 0.0), axis=0, keepdims=True), e_pos)
        r_t = [r_ref[p, ds, :] * e_pos[p] for p in P]
        kk = [kk_ref[p, ds, :] for p in P]
        a_t = each(lambda x, e: -x * e, kk, e_exc)
        b_t = [kk[p] * a_ref[p, ds, :] * e_neg[p] for p in P]
        k_t = [k_ref[p, ds, :] * e_neg[p] for p in P]
        v = [v_ref[p, ds, :] for p in P]
        gram = each(lambda at, rt, bt, kt: _dot_nt(
            bf(jnp.concatenate([at, rt], axis=0)),
            bf(jnp.concatenate([_bd_rows(bt, lane_lo), _bd_rows(kt, lane_lo)], axis=0))),
            a_t, r_t, b_t, k_t)
        l_ab = each(lambda g: jnp.where(strict, g[:c, :LANES], 0.0), gram)
        l_ak = each(lambda g: jnp.where(strict, g[:c, LANES:], 0.0), gram)
        p_rb = each(lambda g: jnp.where(incl, g[c:, :LANES], 0.0), gram)
        p_rk = each(lambda g: jnp.where(incl, g[c:, LANES:], 0.0), gram)

        n0 = each(lambda x: jnp.where(blk[0], x, 0.0), l_ab)
        p1 = each(pmul, n0, n0)
        t0 = each(lambda x: eye_f + x, n0)
        both = each(lambda t, q: pmul(jnp.concatenate([t, q], axis=0), q), t0, p1)
        t1 = each(lambda t, bo: t + bo[:c], t0, both)
        tinv = each(lambda t, bo: t + pmul(t, bo[c:]), t1, both)
        prev_blk = blk[0]
        for nb in (blk[1], blk[2], None):
            cur = strict if nb is None else (strict & nb)
            sel = cur & jnp.logical_not(prev_blk)
            mid = each(lambda x, t: pmul(jnp.where(sel, x, 0.0), t), l_ab, tinv)
            tinv = each(lambda t, m: t + pmul(t, m), tinv, mid)
            prev_blk = nb

        st = [st_ref[p] for p in P]
        st_b = each(bf, st)
        v_bd = each(lambda x: bf(_bd_rows(x, lane_lo)), v)
        rhs_u = each(lambda at, lk, sb, vb: _dot(bf(jnp.concatenate([at, lk], axis=1)),
                                                 jnp.concatenate([sb, vb], axis=0)),
                     a_t, l_ak, st_b, v_bd)
        u_p = each(pmul, tinv, rhs_u)
        o = each(lambda rt, prb, prk, sb, up, vb: _dot(
            bf(jnp.concatenate([rt, prb, prk], axis=1)),
            jnp.concatenate([sb, bf(_bd_rows(up, lane_lo)), vb], axis=0)),
            r_t, p_rb, p_rk, st_b, u_p, v_bd)

        upd = each(lambda bt, kt, pe, up, vv: _dot(
            bf(jnp.concatenate([bt * pe, kt * pe], axis=0).T),
            bf(jnp.concatenate([up, vv], axis=0))),
            b_t, k_t, p_end, u_p, v)
        for p in P:
            p_col = jnp.sum(jnp.where(diag, p_end[p], 0.0), axis=1, keepdims=True)
            st_ref[p] = st[p] * p_col + jnp.where(same_head, upd[p], 0.0)

        mu = each(lambda x: _dot(jnp.concatenate(_split2(x), axis=1), bf(bd_avg2)), o)
        dlt = each(lambda x, m: x - m, o, mu)
        var = each(lambda x: _dot(jnp.concatenate(_split2(x * x), axis=1), bf(bd_avg2)), dlt)
        for p in P:
            o_ref[0, p, ds, :] = dlt[p] * lax.rsqrt(var[p] + GN_EPS)
        return carry

    lax.fori_loop(0, n_chunk, chunk_body, 0)


def _wkv_scan(geom, r, k, v, kk, a, lw):
    B, L, C = geom.B, geom.L, geom.C
    blk = SCAN_BLOCK
    n_lat_blk = L // blk
    n_ctx_blk = C // blk
    n_steps = n_lat_blk + n_ctx_blk
    lat_blocks = geom.n_lat // blk

    def row_block(d, b, j):
        in_ctx = j < n_ctx_blk
        jc = jnp.where(d == 0, j, n_ctx_blk - 1 - j)
        jl = jnp.where(d == 0, j - n_ctx_blk, n_steps - 1 - j)
        return jnp.where(in_ctx, lat_blocks + b * n_ctx_blk + jc, b * n_lat_blk + jl)

    spec = pl.BlockSpec((N_SLAB, blk, LANES), lambda d, b, j: (0, row_block(d, b, j), 0))
    dir_spec = pl.BlockSpec((1, N_SLAB, blk, LANES), lambda d, b, j: (d, 0, row_block(d, b, j), 0))
    kern = functools.partial(_wkv_kernel, n_chunk=blk // CHUNK)
    return pl.pallas_call(
        kern, grid=(2, B, n_steps),
        in_specs=[spec, spec, spec, spec, spec, dir_spec],
        out_specs=dir_spec,
        out_shape=jax.ShapeDtypeStruct((2, N_SLAB, geom.rows, LANES), F32),
        scratch_shapes=[pltpu.VMEM((N_SLAB, LANES, LANES), F32)],
        compiler_params=_cparams(("arbitrary", "arbitrary", "arbitrary")),
        name="wkv_scan",
    )(r, k, v, kk, a, lw)


def _rwkv_out_kernel(on_ref, r_ref, k_ref, v_ref, g_ref, vec_ref, w_ref, x_ref, mod_ref,
                     o_ref, h_ref):
    j = pl.program_id(1)

    @pl.when(j == 0)
    def _():
        ones_bd = _head_sum_mat()
        ones_bd2 = jnp.concatenate([ones_bd, ones_bd], axis=0)
        vec = vec_ref[...]
        for p in range(N_SLAB):
            sl = slice(p * LANES, (p + 1) * LANES)
            rk, lg, lb = vec[0:1, sl], vec[1:2, sl], vec[2:3, sl]
            bonus = _head_sum(r_ref[p] * k_ref[p] * rk, ones_bd2) * v_ref[p]
            y = ((on_ref[0, p] * lg + lb + bonus) * g_ref[0, p]
                 + (on_ref[1, p] * lg + lb + bonus) * g_ref[1, p])
            h_ref[:, sl] = y.astype(BF16)

    acc = _dot(h_ref[...], w_ref[...])
    o_ref[...] = x_ref[...] + mod_ref[0][2:3] * acc


def _rwkv_out(geom, on, r, k, v, g, vec, w_o, x, mod):
    tm, tn = 256, 512
    R = geom.rows
    pair_spec = pl.BlockSpec((N_SLAB, tm, LANES), lambda i, j: (0, i, 0))
    dir_spec = pl.BlockSpec((2, N_SLAB, tm, LANES), lambda i, j: (0, 0, i, 0))
    return pl.pallas_call(
        _rwkv_out_kernel, grid=(R // tm, D_MODEL // tn),
        in_specs=[dir_spec, pair_spec, pair_spec, pair_spec, dir_spec,
                  _const_spec(vec.shape),
                  pl.BlockSpec((D_MODEL, tn), lambda i, j: (0, j)),
                  pl.BlockSpec((tm, tn), lambda i, j: (i, j)),
                  pl.BlockSpec((1, MOD_ROWS, tn), lambda i, j: (geom.seg(i, tm), 0, j))],
        out_specs=pl.BlockSpec((tm, tn), lambda i, j: (i, j)),
        out_shape=jax.ShapeDtypeStruct((R, D_MODEL), F32),
        scratch_shapes=[pltpu.VMEM((tm, D_MODEL), BF16)],
        compiler_params=_cparams(("arbitrary", "arbitrary")),
        name="rwkv_out",
    )(on, r, k, v, g, vec, w_o, x, mod)


def _attn_kernel(*refs, tq, tk, n_lat_chunks, lambda_init):
    if n_lat_chunks:
        q_ref, kc_ref, vc_ref, kl_ref, vl_ref, lam_ref, sg_ref, o_ref, m_ref, l_ref, acc_ref = refs
    else:
        q_ref, kc_ref, vc_ref, lam_ref, sg_ref, o_ref, m_ref, l_ref, acc_ref = refs
    lane = lax.broadcasted_iota(jnp.int32, (tq, LANES), 1)
    q = q_ref[...]
    scale = HEAD_DIM ** -0.5
    zero = jnp.zeros_like(q)
    qs = (jnp.where(lane < HEAD_DIM, q, zero) * scale, jnp.where(lane < HEAD_DIM, zero, q) * scale)

    m_ref[...] = jnp.full_like(m_ref, -jnp.inf)
    l_ref[...] = jnp.zeros_like(l_ref)
    acc_ref[...] = jnp.zeros_like(acc_ref)

    def scores(k):
        return [_dot_nt(qs[s], k) for s in range(2)]

    def consume(sc, v):
        n_t = sc[0].shape[1] // LANES
        tiles = [[sc[s][:, t * LANES:(t + 1) * LANES] for t in range(n_t)] for s in range(2)]
        m_old = [m_ref[s] for s in range(2)]
        m_new = [jnp.maximum(m_old[s], jnp.max(functools.reduce(jnp.maximum, tiles[s]),
                                               axis=-1, keepdims=True)) for s in range(2)]
        alpha = [jnp.exp(m_old[s] - m_new[s]) for s in range(2)]
        pr = [[jnp.exp(t - m_new[s]) for t in tiles[s]] for s in range(2)]
        for s in range(2):
            l_ref[s] = alpha[s] * l_ref[s] + functools.reduce(lambda x, y: x + y, pr[s])
            m_ref[s] = m_new[s]
        pv = [_dot(jnp.concatenate(pr[s], axis=1).astype(BF16), v) for s in range(2)]
        for s in range(2):
            acc_ref[s] = alpha[s] * acc_ref[s] + pv[s]

    consume(scores(kc_ref[...]), vc_ref[...])

    if n_lat_chunks:
        def body(ci, sc):
            nxt = pl.ds(pl.multiple_of((ci + 1) * tk, tk), tk)
            sc_next = scores(kl_ref[nxt, :])
            consume(sc, vl_ref[pl.ds(pl.multiple_of(ci * tk, tk), tk), :])
            return sc_next

        sc_last = lax.fori_loop(0, n_lat_chunks - 1, body, scores(kl_ref[0:tk, :]))
        consume(sc_last, vl_ref[(n_lat_chunks - 1) * tk:n_lat_chunks * tk, :])

    lv = lam_ref[...]
    lam = (jnp.exp(jnp.sum(lv[0:1] * lv[1:2], axis=-1, keepdims=True))
           - jnp.exp(jnp.sum(lv[2:3] * lv[3:4], axis=-1, keepdims=True)) + lambda_init)
    l_tot = [jnp.sum(l_ref[s], axis=-1, keepdims=True) for s in range(2)]
    o = acc_ref[0] / l_tot[0] - lam * (acc_ref[1] / l_tot[1])
    o = o * lax.rsqrt(jnp.mean(o * o, axis=-1, keepdims=True) + SUBLN_EPS) * sg_ref[0:1]
    o_ref[...] = (o * (1.0 - lambda_init)).astype(o_ref.dtype)


def _attention(geom, qkv, lam_vec, subln_g, lambda_init, *, need_ctx):
    B, L, C = geom.B, geom.L, geom.C
    n_heads = D_MODEL // LANES
    ctx_row_blk = geom.n_lat // C
    kc_spec = pl.BlockSpec((C, LANES), lambda b, h, qi: (ctx_row_blk + b, n_heads + h))
    vc_spec = pl.BlockSpec((C, LANES), lambda b, h, qi: (ctx_row_blk + b, 2 * n_heads + h))
    small = [_const_spec(lam_vec.shape), _const_spec(subln_g.shape)]

    def call(name, tq, q_map, n_q, tk, n_lat_chunks, extra_specs, rows_out):
        kern = functools.partial(_attn_kernel, tq=tq, tk=tk, n_lat_chunks=n_lat_chunks,
                                 lambda_init=lambda_init)
        n_kv = 2 + len(extra_specs)
        return pl.pallas_call(
            kern, grid=(B, n_heads, n_q),
            in_specs=[pl.BlockSpec((tq, LANES), q_map), kc_spec, vc_spec] + extra_specs + small,
            out_specs=pl.BlockSpec((tq, LANES), lambda b, h, qi: (q_map(b, h, qi)[0] - rows_out[0], h)),
            out_shape=jax.ShapeDtypeStruct((rows_out[1], D_MODEL), BF16),
            scratch_shapes=[pltpu.VMEM((2, tq, LANES), F32)] * 3,
            compiler_params=_cparams(("arbitrary", "arbitrary", "arbitrary")),
            name=name,
        )(*([qkv] * (1 + n_kv) + [lam_vec, subln_g]))

    tq = min(ATTN_TQ, L)
    tk = min(ATTN_TK, L)
    lat_specs = [pl.BlockSpec((L, LANES), lambda b, h, qi: (b, n_heads + h)),
                 pl.BlockSpec((L, LANES), lambda b, h, qi: (b, 2 * n_heads + h))]
    o_lat = call("diff_attention", tq, lambda b, h, qi: (b * (L // tq) + qi, h), L // tq, tk, L // tk,
                 lat_specs, (0, geom.n_lat))
    if not need_ctx:
        return o_lat
    o_ctx = call("diff_attention_ctx", C, lambda b, h, qi: (ctx_row_blk + b, h), 1, tk, 0, [],
                 (ctx_row_blk, geom.n_ctx))
    return jnp.concatenate([o_lat, o_ctx], axis=0)


def _rope_tables(geom):
    L = geom.L
    t = jnp.arange(L, dtype=jnp.int32)
    inv = ROPE_THETA ** (-jnp.arange(ROPE_PAIRS, dtype=F32) / ROPE_PAIRS)
    lane = jnp.arange(LANES, dtype=jnp.int32) % HEAD_DIM
    freq = inv[lane % ROPE_PAIRS]
    pos = jnp.where((lane < HEAD_DIM // 2)[None, :], (t // GRID_W)[:, None], (t % GRID_W)[:, None])
    ang = pos.astype(F32) * freq[None, :]
    sign = jnp.where((lane % 32) < 16, -1.0, 1.0)
    cos_l = jnp.cos(ang)
    sin_l = jnp.sin(ang) * sign[None, :]
    cos = jnp.concatenate([jnp.tile(cos_l, (geom.B, 1)), jnp.ones((geom.n_ctx, LANES), F32)], axis=0)
    sin = jnp.concatenate([jnp.tile(sin_l, (geom.B, 1)), jnp.zeros((geom.n_ctx, LANES), F32)], axis=0)
    return cos, sin


def _pad_cols(w, n):
    return jnp.pad(w, ((0, 0), (0, n - w.shape[1])))


def _pad_rows(w, n, offset=0):
    return jnp.pad(w, ((offset, n - offset - w.shape[0]), (0, 0)))


def _rwkv_weights(j, norm_g_row, rw_mix, rw_w_rkv, rw_w0, rw_w1, rw_w2, rw_a0, rw_a1, rw_a2, rw_g1,
                  rw_g2, rw_kk, rw_ka, rw_v0, rw_v1, rw_v2):
    lora_w = rw_w1.shape[-1]
    lora_g = rw_g1.shape[-1]
    g_pad = 3 * LANES
    wts = {
        "norm_g": norm_g_row.reshape(1, D_MODEL),
        "mix": jnp.pad(rw_mix[j], ((0, 2), (0, 0))),
        "wrkv": rw_w_rkv[j].astype(BF16),
        "w1": jnp.concatenate([rw_w1[j, 0], rw_w1[j, 1]], axis=1).astype(BF16),
        "w2f": _pad_rows(rw_w2[j, 0], LANES).astype(BF16),
        "w2b": _pad_rows(rw_w2[j, 1], LANES, lora_w).astype(BF16),
        "g1": _pad_cols(jnp.concatenate([rw_g1[j, 0], rw_g1[j, 1]], axis=1), g_pad).astype(BF16),
        "g2f": _pad_rows(rw_g2[j, 0], g_pad).astype(BF16),
        "g2b": _pad_rows(rw_g2[j, 1], g_pad, lora_g).astype(BF16),
        "a1": _pad_cols(rw_a1[j], LANES).astype(BF16),
        "a2": _pad_rows(rw_a2[j], LANES).astype(BF16),
    }
    v0 = rw_v0[j - 1] if j > 0 else jnp.zeros((D_MODEL,), F32)
    wts["vec"] = jnp.stack([rw_w0[j, 0], rw_w0[j, 1], rw_a0[j], rw_kk[j], rw_ka[j], v0,
                            jnp.zeros((D_MODEL,), F32), jnp.zeros((D_MODEL,), F32)], axis=0)
    if j > 0:
        wts["v1"] = _pad_cols(rw_v1[j - 1], LANES).astype(BF16)
        wts["v2"] = _pad_rows(rw_v2[j - 1], LANES).astype(BF16)
    return wts


def kernel(x, c, ctx, c_ctx, ada_w, ada_b, norm_g, final_g, rw_mix, rw_w_rkv, rw_w0, rw_w1, rw_w2, rw_a0, rw_a1, rw_a2, rw_g1, rw_g2, rw_kk, rw_ka, rw_rk, rw_ln_g, rw_ln_b, rw_w_o, rw_v0, rw_v1, rw_v2, da_w_qkv, da_w_o, da_lq1, da_lk1, da_lq2, da_lk2, da_subln_g, mlp_w1, mlp_w2):
    B, L, D = x.shape
    C = ctx.shape[1]
    depth = ada_w.shape[0]
    geom = _Geom(B, L, C)
    tm = geom.tm

    n_seg = 16
    cc = jnp.concatenate([c, c_ctx[None, :], jnp.zeros((n_seg - B - 1, D), F32)], axis=0)
    mod_all = _ada_table(cc, ada_w, ada_b).reshape(depth, n_seg, 6, D)
    mod_all = jnp.pad(mod_all, ((0, 0), (0, 0), (0, MOD_ROWS - 6), (0, 0)))

    xs = jnp.concatenate([x.reshape(B * L, D), ctx.reshape(B * C, D)], axis=0)
    rope_cos, rope_sin = _rope_tables(geom)
    vfirst = None

    for i in range(depth):
        last = i == depth - 1
        mod = mod_all[i]
        j = i // 2
        ng0 = norm_g[i, 0].reshape(1, D)
        ng1 = norm_g[i, 1].reshape(1, D)
        n_rows = geom.n_lat if last else geom.rows
        tn = 512
        xtile = pl.BlockSpec((tm, tn), lambda ti, tj: (ti, tj))
        if i % 2 == 0:
            wts = _rwkv_weights(j, norm_g[i, 0], rw_mix, rw_w_rkv, rw_w0, rw_w1, rw_w2, rw_a0, rw_a1,
                                rw_a2, rw_g1, rw_g2, rw_kk, rw_ka, rw_v0, rw_v1, rw_v2)
            r, k, v, kk, a, lw, g = _rwkv_features(geom, xs, mod, wts, vfirst)
            if j == 0:
                vfirst = v
            on = _wkv_scan(geom, r, k, v, kk, a, lw)
            vec = jnp.stack([rw_rk[j].reshape(D), rw_ln_g[j], rw_ln_b[j]]
                            + [jnp.zeros((D,), F32)] * 5, axis=0)
            xs = _rwkv_out(geom, on, r, k, v, g, vec, rw_w_o[j].astype(BF16), xs, mod)
        else:
            lambda_init = 0.8 - 0.6 * math.exp(-0.3 * i)
            rope_spec = pl.BlockSpec((tm, LANES), lambda ti, tj: (ti, 0))
            qkv = _mm_call("attn_qkv", xs, da_w_qkv[j].astype(BF16), n_rows=geom.rows, tm=tm, tn=tn,
                           prologue=_normmod_prologue(0, 1),
                           pro_inputs=[(mod, _mod_full_spec(geom, tm)), (ng0, _const_spec(ng0.shape))],
                           epilogue=_rope_epilogue(2 * D // tn, tn),
                           epi_inputs=[(rope_cos, rope_spec), (rope_sin, rope_spec)],
                           out_dtype=BF16)
            lam_vec = jnp.pad(jnp.stack([da_lq1[j], da_lk1[j], da_lq2[j], da_lk2[j]], axis=0),
                              ((0, 4), (0, LANES - HEAD_DIM)))
            sg = da_subln_g[j].reshape(1, LANES)
            o_att = _attention(geom, qkv, lam_vec, sg, lambda_init, need_ctx=not last)
            xs = _mm_call("attn_out", o_att, da_w_o[j].astype(BF16), n_rows=n_rows, tm=tm, tn=tn,
                          prologue=None, pro_inputs=[], epilogue=_gated_residual_epilogue(2),
                          epi_inputs=[(xs, xtile), (mod, _mod_tile_spec(geom, tm, tn))],
                          out_dtype=F32)
        hmid = _mm_call("mlp_up", xs, mlp_w1[i].astype(BF16), n_rows=n_rows, tm=tm, tn=tn,
                        prologue=_normmod_prologue(3, 4),
                        pro_inputs=[(mod, _mod_full_spec(geom, tm)), (ng1, _const_spec(ng1.shape))],
                        epilogue=_relu2_epilogue, epi_inputs=[], out_dtype=BF16)
        if last:
            tmf = min(tm, 512)
            fg = final_g.reshape(1, D)
            xs = _mm_call("mlp_down_final", hmid, mlp_w2[i].astype(BF16), n_rows=n_rows, tm=tmf, tn=D,
                          prologue=None, pro_inputs=[], epilogue=_gated_residual_norm_epilogue(5),
                          epi_inputs=[(xs, pl.BlockSpec((tmf, D), lambda ti, tj: (ti, 0))),
                                      (mod, _mod_tile_spec(geom, tmf, D)),
                                      (fg, _const_spec(fg.shape))],
                          out_dtype=F32)
        else:
            xs = _mm_call("mlp_down", hmid, mlp_w2[i].astype(BF16), n_rows=n_rows, tm=tm, tn=tn,
                          prologue=None, pro_inputs=[], epilogue=_gated_residual_epilogue(5),
                          epi_inputs=[(xs, xtile), (mod, _mod_tile_spec(geom, tm, tn))],
                          out_dtype=F32)
    return xs[:B * L].reshape(B, L, D)
```

```python
import functools
import math

import jax
import jax.numpy as jnp
from jax import lax
from jax.experimental import pallas as pl
from jax.experimental.pallas import tpu as pltpu

D_MODEL = 1024
HEAD_DIM = 64
LANES = 128
N_SLAB = D_MODEL // LANES
D_FF = 4 * D_MODEL
GRID_W = 64
ROPE_THETA = 10000.0
ROPE_PAIRS = HEAD_DIM // 4
NORM_EPS = 1e-6
SUBLN_EPS = 1e-5
GN_EPS = 64e-5
CHUNK = 64
SCAN_BLOCK = 256
ATTN_TQ = 512
ATTN_TK = 512
ATTN_ROWS = 64
MOD_ROWS = 8
VMEM_LIMIT = 56 * 1024 * 1024

F32 = jnp.float32
BF16 = jnp.bfloat16


def _cparams(sem):
    return pltpu.CompilerParams(dimension_semantics=sem, vmem_limit_bytes=VMEM_LIMIT)


def _const_spec(shape):
    nd = len(shape)
    return pl.BlockSpec(shape, lambda *_: (0,) * nd, pipeline_mode=pl.Buffered(1))


def _dot(a, b):
    return jnp.dot(a, b, preferred_element_type=F32)


def _dot_nt(a, b):
    return lax.dot_general(a, b, (((1,), (1,)), ((), ())), preferred_element_type=F32)


def _split2(x):
    hi = x.astype(BF16)
    lo = (x - hi.astype(F32)).astype(BF16)
    return hi, lo


def _split3(x):
    hi = x.astype(BF16)
    r1 = x - hi.astype(F32)
    mid = r1.astype(BF16)
    lo = (r1 - mid.astype(F32)).astype(BF16)
    return hi, mid, lo


def _rms_mod(x, g, shift, scale):
    y = x * lax.rsqrt(jnp.mean(x * x, axis=-1, keepdims=True) + NORM_EPS) * g
    return y * (1.0 + scale) + shift


def _head_sum_mat():
    r = lax.broadcasted_iota(jnp.int32, (LANES, LANES), 0) // HEAD_DIM
    c = lax.broadcasted_iota(jnp.int32, (LANES, LANES), 1) // HEAD_DIM
    return (r == c).astype(BF16)


def _head_sum(x, ones_bd2):
    hi, lo = _split2(x)
    return _dot(jnp.concatenate([hi, lo], axis=1), ones_bd2)


def _ada_kernel(cc_ref, w_ref, b_ref, o_ref):
    cc = cc_ref[...]
    s = cc * jax.nn.sigmoid(cc)
    o_ref[0] = jnp.dot(s, w_ref[0], preferred_element_type=F32,
                       precision=lax.Precision.HIGHEST) + b_ref[0]


def _ada_table(cc, ada_w, ada_b):
    depth, d, n = ada_w.shape
    rows = cc.shape[0]
    tn = 1536
    return pl.pallas_call(
        _ada_kernel,
        grid=(depth, n // tn),
        in_specs=[pl.BlockSpec((rows, d), lambda i, j: (0, 0)),
                  pl.BlockSpec((1, d, tn), lambda i, j: (i, 0, j)),
                  pl.BlockSpec((1, 1, tn), lambda i, j: (i, 0, j))],
        out_specs=pl.BlockSpec((1, rows, tn), lambda i, j: (i, 0, j)),
        out_shape=jax.ShapeDtypeStruct((depth, rows, n), F32),
        compiler_params=_cparams(("arbitrary", "arbitrary")),
        name="ada_table",
    )(cc, ada_w, ada_b.reshape(depth, 1, n))


class _Geom:
    def __init__(self, B, L, C):
        self.B, self.L, self.C = B, L, C
        self.n_lat = B * L
        self.n_ctx = B * C
        self.rows = self.n_lat + self.n_ctx
        tm = 1024
        while (L % tm) or (self.n_ctx % tm):
            tm //= 2
        self.tm = tm

    def seg(self, i, tm):
        r0 = i * tm
        return jnp.where(r0 < self.n_lat, r0 // self.L, self.B)


def _mm_kernel(*refs, n_pro, n_epi, prologue, epilogue):
    x_ref = refs[0]
    pro_refs = refs[1:1 + n_pro]
    w_ref = refs[1 + n_pro]
    epi_refs = refs[2 + n_pro:2 + n_pro + n_epi]
    o_ref = refs[2 + n_pro + n_epi]
    j = pl.program_id(1)
    if prologue is not None:
        h_ref = refs[3 + n_pro + n_epi]

        @pl.when(j == 0)
        def _():
            h_ref[...] = prologue(x_ref, pro_refs).astype(BF16)

        lhs = h_ref[...]
    else:
        lhs = x_ref[...]
    acc = _dot(lhs, w_ref[...])
    epilogue(o_ref, acc, epi_refs, j)


def _mm_call(name, x, w, *, n_rows, tm, tn, prologue, pro_inputs, epilogue, epi_inputs,
             out_dtype):
    K, N = w.shape
    in_specs = [pl.BlockSpec((tm, K), lambda i, j: (i, 0))]
    args = [x]
    for a, spec in pro_inputs:
        in_specs.append(spec)
        args.append(a)
    in_specs.append(pl.BlockSpec((K, tn), lambda i, j: (0, j)))
    args.append(w)
    for a, spec in epi_inputs:
        in_specs.append(spec)
        args.append(a)
    scratch = [pltpu.VMEM((tm, K), BF16)] if prologue is not None else []
    kern = functools.partial(_mm_kernel, n_pro=len(pro_inputs), n_epi=len(epi_inputs),
                             prologue=prologue, epilogue=epilogue)
    return pl.pallas_call(
        kern, grid=(n_rows // tm, N // tn), in_specs=in_specs,
        out_specs=pl.BlockSpec((tm, tn), lambda i, j: (i, j)),
        out_shape=jax.ShapeDtypeStruct((n_rows, N), out_dtype),
        scratch_shapes=scratch,
        compiler_params=_cparams(("arbitrary", "arbitrary")),
        name=name,
    )(*args)


def _mod_full_spec(geom, tm):
    return pl.BlockSpec((1, MOD_ROWS, D_MODEL), lambda i, j: (geom.seg(i, tm), 0, 0))


def _mod_tile_spec(geom, tm, tn):
    return pl.BlockSpec((1, MOD_ROWS, tn), lambda i, j: (geom.seg(i, tm), 0, j))


def _normmod_prologue(shift_row, scale_row):
    def prologue(x_ref, pro_refs):
        mod_ref, g_ref = pro_refs
        m = mod_ref[0]
        return _rms_mod(x_ref[...], g_ref[...], m[shift_row:shift_row + 1],
                        m[scale_row:scale_row + 1])
    return prologue


def _rope_epilogue(n_rope_tiles, tn):
    def epilogue(o_ref, acc, epi_refs, j):
        cos_ref, sin_ref = epi_refs

        @pl.when(j < n_rope_tiles)
        def _():
            f = jnp.where(j < n_rope_tiles // 2, HEAD_DIM ** -0.5 * math.log2(math.e), 1.0)
            cos = cos_ref[...] * f
            sin = sin_ref[...] * f
            lane = lax.broadcasted_iota(jnp.int32, cos.shape, 1)
            low = (lane & 16) == 0
            for s in range(tn // LANES):
                xs = acc[:, s * LANES:(s + 1) * LANES]
                partner = jnp.where(low, pltpu.roll(xs, LANES - 16, 1), pltpu.roll(xs, 16, 1))
                o_ref[:, s * LANES:(s + 1) * LANES] = (xs * cos + partner * sin).astype(o_ref.dtype)

        @pl.when(j >= n_rope_tiles)
        def _():
            o_ref[...] = acc.astype(o_ref.dtype)
    return epilogue


def _gated_residual_epilogue(gate_row):
    def epilogue(o_ref, acc, epi_refs, j):
        x_ref, mod_ref = epi_refs
        gate = mod_ref[0][gate_row:gate_row + 1]
        o_ref[...] = x_ref[...] + gate * acc
    return epilogue


def _gated_residual_norm_epilogue(gate_row):
    def epilogue(o_ref, acc, epi_refs, j):
        x_ref, mod_ref, g_ref = epi_refs
        gate = mod_ref[0][gate_row:gate_row + 1]
        x = x_ref[...] + gate * acc
        o_ref[...] = x * lax.rsqrt(jnp.mean(x * x, axis=-1, keepdims=True) + NORM_EPS) * g_ref[...]
    return epilogue


def _relu2_epilogue(o_ref, acc, epi_refs, j):
    a = jnp.maximum(acc, 0.0)
    o_ref[...] = (a * a).astype(o_ref.dtype)


def _rwkv_feat_kernel(*refs, tm, seg_lat, seg_ctx, n_lat, has_vres):
    (x_ref, xp_ref, xn_ref, mod_ref, ng_ref, mix_ref, wrkv_ref, w1_ref, w2f_ref, w2b_ref,
     g1_ref, g2f_ref, g2b_ref, a1_ref, a2_ref, vec_ref) = refs[:16]
    pos = 16
    if has_vres:
        v1_ref, v2_ref, vfirst_ref = refs[pos:pos + 3]
        pos += 3
    (r_o, k_o, v_o, kk_o, a_o, lw_o, g_o) = refs[pos:pos + 7]

    i = pl.program_id(0)
    m = mod_ref[0]
    shift, scale = m[0:1], m[1:2]
    g = ng_ref[...]
    h = _rms_mod(x_ref[...], g, shift, scale)
    hp = _rms_mod(xp_ref[...], g, shift, scale)[7:8]
    hn = _rms_mod(xn_ref[...], g, shift, scale)[0:1]

    row = lax.broadcasted_iota(jnp.int32, (tm, 1), 0)
    rid = row + i * tm
    in_lat = rid < n_lat
    seg_pos = jnp.where(in_lat, rid % seg_lat, (rid - n_lat) % seg_ctx)
    seg_len = jnp.where(in_lat, seg_lat, seg_ctx)
    prev = pltpu.roll(h, 1, 0)
    prev = jnp.where(row == 0, hp, prev)
    prev = jnp.where(seg_pos == 0, 0.0, prev)
    nxt = pltpu.roll(h, tm - 1, 0)
    nxt = jnp.where(row == tm - 1, hn, nxt)
    nxt = jnp.where(seg_pos == seg_len - 1, 0.0, nxt)
    xx = 0.5 * (prev + nxt) - h

    def mixed(mi):
        return (h + xx * mix_ref[mi:mi + 1]).astype(BF16)

    vec = vec_ref[...]
    ones_bd = _head_sum_mat()
    ones_bd2 = jnp.concatenate([ones_bd, ones_bd], axis=0)

    def store(o_ref, val):
        for p in range(N_SLAB):
            o_ref[p] = val[:, p * LANES:(p + 1) * LANES]

    xr = mixed(0)
    store(r_o, _dot(xr, wrkv_ref[0]))

    xw = mixed(1)
    tw = jnp.tanh(_dot(xw, w1_ref[...])).astype(BF16)
    neg_e = -math.exp(-0.5)
    store(lw_o.at[0], neg_e * jax.nn.sigmoid(vec[0:1] + _dot(tw, w2f_ref[...])))
    store(lw_o.at[1], neg_e * jax.nn.sigmoid(vec[1:2] + _dot(tw, w2b_ref[...])))

    xg = mixed(5)
    sg = jax.nn.sigmoid(_dot(xg, g1_ref[...])).astype(BF16)
    store(g_o.at[0], _dot(sg, g2f_ref[...]))
    store(g_o.at[1], _dot(sg, g2b_ref[...]))

    xa = mixed(4)
    a = jax.nn.sigmoid(vec[2:3] + _dot(_dot(xa, a1_ref[...]).astype(BF16), a2_ref[...]))
    store(a_o, a)

    xv = mixed(3)
    v = _dot(xv, wrkv_ref[2])
    if has_vres:
        gate = jax.nn.sigmoid(vec[5:6] + _dot(_dot(xv, v1_ref[...]).astype(BF16), v2_ref[...]))
        for p in range(N_SLAB):
            sl = slice(p * LANES, (p + 1) * LANES)
            vp = v[:, sl]
            v_o[p] = vp + (vfirst_ref[p] - vp) * gate[:, sl]
    else:
        store(v_o, v)

    xk = mixed(2)
    k = _dot(xk, wrkv_ref[1])
    k_scale = 1.0 + (a - 1.0) * vec[4:5]
    kraw = k * vec[3:4]
    for p in range(N_SLAB):
        sl = slice(p * LANES, (p + 1) * LANES)
        kr = kraw[:, sl]
        ss = _head_sum(kr * kr, ones_bd2)
        kk_o[p] = kr * lax.rsqrt(jnp.maximum(ss, 1e-24))
        k_o[p] = k[:, sl] * k_scale[:, sl]


def _rwkv_features(geom, x, mod, wts, vfirst):
    tm = 256
    R = geom.rows
    has_vres = vfirst is not None
    nblk8 = R // 8
    pm = functools.partial
    in_specs = [
        pl.BlockSpec((tm, D_MODEL), lambda i: (i, 0)),
        pl.BlockSpec((8, D_MODEL), lambda i: (jnp.maximum(i * (tm // 8) - 1, 0), 0)),
        pl.BlockSpec((8, D_MODEL), lambda i: (jnp.minimum((i + 1) * (tm // 8), nblk8 - 1), 0)),
        pl.BlockSpec((1, MOD_ROWS, D_MODEL), lambda i: (geom.seg(i, tm), 0, 0)),
    ]
    args = [x, x, x, mod]
    names = ["norm_g", "mix", "wrkv", "w1", "w2f", "w2b", "g1", "g2f", "g2b", "a1", "a2", "vec"]
    if has_vres:
        names += ["v1", "v2"]
    for nme in names:
        a = wts[nme]
        in_specs.append(_const_spec(a.shape))
        args.append(a)
    pair_spec = pl.BlockSpec((N_SLAB, tm, LANES), lambda i: (0, i, 0))
    dir_spec = pl.BlockSpec((2, N_SLAB, tm, LANES), lambda i: (0, 0, i, 0))
    if has_vres:
        in_specs.append(pair_spec)
        args.append(vfirst)
    pair_shape = jax.ShapeDtypeStruct((N_SLAB, R, LANES), F32)
    dir_shape = jax.ShapeDtypeStruct((2, N_SLAB, R, LANES), F32)
    kern = pm(_rwkv_feat_kernel, tm=tm, seg_lat=geom.L, seg_ctx=geom.C, n_lat=geom.n_lat,
              has_vres=has_vres)
    return pl.pallas_call(
        kern, grid=(R // tm,), in_specs=in_specs,
        out_specs=[pair_spec] * 5 + [dir_spec] * 2, out_shape=[pair_shape] * 5 + [dir_shape] * 2,
        compiler_params=_cparams(("arbitrary",)),
        name="rwkv_features",
    )(*args)


def _bd_rows(x, lane_lo):
    return jnp.concatenate([jnp.where(lane_lo, x, 0.0), jnp.where(lane_lo, 0.0, x)], axis=0)


def _wkv_kernel(r_ref, k_ref, v_ref, kk_ref, a_ref, lw_ref, o_ref, st_ref, *, n_chunk):
    d = pl.program_id(0)
    j = pl.program_id(2)
    c = CHUNK

    @pl.when(j == 0)
    def _():
        st_ref[...] = jnp.zeros_like(st_ref)

    sign = 1 - 2 * d
    t_i = lax.broadcasted_iota(jnp.int32, (c, LANES), 0)
    lane = lax.broadcasted_iota(jnp.int32, (c, LANES), 1)
    s_i = lane % c
    lane_lo = lane < c
    u = (t_i - s_i) * sign
    strict = u > 0
    incl = u >= 0
    eye = u == 0
    blk = [(t_i // n) == (s_i // n) for n in (8, 16, 32)]
    tc = lax.broadcasted_iota(jnp.int32, (c, 3 * c), 0)
    sc = lax.broadcasted_iota(jnp.int32, (c, 3 * c), 1) % c
    mcum = (((tc - sc) * sign) >= 0).astype(BF16)
    last_row = jnp.where(d == 0, c - 1, 0)
    row1 = lax.broadcasted_iota(jnp.int32, (c, 1), 0)
    is_last = row1 == last_row
    bd_avg = _head_sum_mat() * (1.0 / HEAD_DIM)
    bd_avg2 = jnp.concatenate([bd_avg, bd_avg], axis=0)
    kr = lax.broadcasted_iota(jnp.int32, (LANES, LANES), 0)
    kc = lax.broadcasted_iota(jnp.int32, (LANES, LANES), 1)
    same_head = (kr // HEAD_DIM) == (kc // HEAD_DIM)
    diag = kr == kc
    eye_f = jnp.where(eye, 1.0, 0.0)

    def bf(x):
        return x.astype(BF16)

    def pmul(x_pair, y_pair):
        return _dot(bf(x_pair), bf(_bd_rows(y_pair, lane_lo)))

    def chunk_body(ci, carry):
        c_eff = ci + d * (n_chunk - 1 - 2 * ci)
        ds = pl.ds(pl.multiple_of(c_eff * c, c), c)
        P = range(N_SLAB)

        def each(fn, *lists):
            return [fn(*args) for args in zip(*lists)]

        lw = [lw_ref[0, p, ds, :] for p in P]
        cum = each(lambda x: _dot(mcum, jnp.concatenate(_split3(x), axis=0)), lw)
        e_pos = each(jnp.exp, cum)
        e_neg = each(lambda x: jnp.exp(-x), cum)
        e_exc = each(lambda x, y: jnp.exp(x - y), cum, lw)
        p_end = each(lambda e: jnp.sum(jnp.where(is_last, e, 0.0), axis=0, keepdims=True), e_pos)
        r_t = [r_ref[p, ds, :] * e_pos[p] for p in P]
        kk = [kk_ref[p, ds, :] for p in P]
        a_t = each(lambda x, e: -x * e, kk, e_exc)
        b_t = [kk[p] * a_ref[p, ds, :] * e_neg[p] for p in P]
        k_t = [k_ref[p, ds, :] * e_neg[p] for p in P]
        v = [v_ref[p, ds, :] for p in P]
        gram = each(lambda at, rt, bt, kt: _dot_nt(
            bf(jnp.concatenate([at, rt], axis=0)),
            bf(jnp.concatenate([_bd_rows(bt, lane_lo), _bd_rows(kt, lane_lo)], axis=0))),
            a_t, r_t, b_t, k_t)
        l_ab = each(lambda g: jnp.where(strict, g[:c, :LANES], 0.0), gram)
        l_ak = each(lambda g: jnp.where(strict, g[:c, LANES:], 0.0), gram)
        p_rb = each(lambda g: jnp.where(incl, g[c:, :LANES], 0.0), gram)
        p_rk = each(lambda g: jnp.where(incl, g[c:, LANES:], 0.0), gram)

        n0 = each(lambda x: jnp.where(blk[0], x, 0.0), l_ab)
        p1 = each(pmul, n0, n0)
        t0 = each(lambda x: eye_f + x, n0)
        both = each(lambda t, q: pmul(jnp.concatenate([t, q], axis=0), q), t0, p1)
        t1 = each(lambda t, bo: t + bo[:c], t0, both)
        tinv = each(lambda t, bo: t + pmul(t, bo[c:]), t1, both)
        prev_blk = blk[0]
        for nb in (blk[1], blk[2], None):
            cur = strict if nb is None else (strict & nb)
            sel = cur & jnp.logical_not(prev_blk)
            mid = each(lambda x, t: pmul(jnp.where(sel, x, 0.0), t), l_ab, tinv)
            tinv = each(lambda t, m: t + pmul(t, m), tinv, mid)
            prev_blk = nb

        st = [st_ref[p] for p in P]
        st_b = each(bf, st)
        v_bd = each(lambda x: bf(_bd_rows(x, lane_lo)), v)
        rhs_u = each(lambda at, lk, sb, vb: _dot(bf(jnp.concatenate([at, lk], axis=1)),
                                                 jnp.concatenate([sb, vb], axis=0)),
                     a_t, l_ak, st_b, v_bd)
        u_p = each(pmul, tinv, rhs_u)
        o = each(lambda rt, prb, prk, sb, up, vb: _dot(
            bf(jnp.concatenate([rt, prb, prk], axis=1)),
            jnp.concatenate([sb, bf(_bd_rows(up, lane_lo)), vb], axis=0)),
            r_t, p_rb, p_rk, st_b, u_p, v_bd)

        upd = each(lambda bt, kt, pe, up, vv: _dot(
            bf(jnp.concatenate([bt * pe, kt * pe], axis=0).T),
            bf(jnp.concatenate([up, vv], axis=0))),
            b_t, k_t, p_end, u_p, v)
        for p in P:
            p_col = jnp.sum(jnp.where(diag, p_end[p], 0.0), axis=1, keepdims=True)
            st_ref[p] = st[p] * p_col + jnp.where(same_head, upd[p], 0.0)

        mu = each(lambda x: _dot(jnp.concatenate(_split2(x), axis=1), bf(bd_avg2)), o)
        dlt = each(lambda x, m: x - m, o, mu)
        var = each(lambda x: _dot(jnp.concatenate(_split2(x * x), axis=1), bf(bd_avg2)), dlt)
        for p in P:
            o_ref[0, p, ds, :] = dlt[p] * lax.rsqrt(var[p] + GN_EPS)
        return carry

    lax.fori_loop(0, n_chunk, chunk_body, 0)


def _wkv_scan(geom, r, k, v, kk, a, lw):
    B, L, C = geom.B, geom.L, geom.C
    blk = SCAN_BLOCK
    n_lat_blk = L // blk
    n_ctx_blk = C // blk
    n_steps = n_lat_blk + n_ctx_blk
    lat_blocks = geom.n_lat // blk

    def row_block(d, b, j):
        in_ctx = j < n_ctx_blk
        jc = jnp.where(d == 0, j, n_ctx_blk - 1 - j)
        jl = jnp.where(d == 0, j - n_ctx_blk, n_steps - 1 - j)
        return jnp.where(in_ctx, lat_blocks + b * n_ctx_blk + jc, b * n_lat_blk + jl)

    spec = pl.BlockSpec((N_SLAB, blk, LANES), lambda d, b, j: (0, row_block(d, b, j), 0))
    dir_spec = pl.BlockSpec((1, N_SLAB, blk, LANES), lambda d, b, j: (d, 0, row_block(d, b, j), 0))
    kern = functools.partial(_wkv_kernel, n_chunk=blk // CHUNK)
    return pl.pallas_call(
        kern, grid=(2, B, n_steps),
        in_specs=[spec, spec, spec, spec, spec, dir_spec],
        out_specs=dir_spec,
        out_shape=jax.ShapeDtypeStruct((2, N_SLAB, geom.rows, LANES), F32),
        scratch_shapes=[pltpu.VMEM((N_SLAB, LANES, LANES), F32)],
        compiler_params=_cparams(("arbitrary", "arbitrary", "arbitrary")),
        name="wkv_scan",
    )(r, k, v, kk, a, lw)


def _rwkv_out_kernel(on_ref, r_ref, k_ref, v_ref, g_ref, vec_ref, w_ref, x_ref, mod_ref,
                     o_ref, h_ref):
    j = pl.program_id(1)

    @pl.when(j == 0)
    def _():
        ones_bd = _head_sum_mat()
        ones_bd2 = jnp.concatenate([ones_bd, ones_bd], axis=0)
        vec = vec_ref[...]
        for p in range(N_SLAB):
            sl = slice(p * LANES, (p + 1) * LANES)
            rk, lg, lb = vec[0:1, sl], vec[1:2, sl], vec[2:3, sl]
            bonus = _head_sum(r_ref[p] * k_ref[p] * rk, ones_bd2) * v_ref[p]
            y = ((on_ref[0, p] * lg + lb + bonus) * g_ref[0, p]
                 + (on_ref[1, p] * lg + lb + bonus) * g_ref[1, p])
            h_ref[:, sl] = y.astype(BF16)

    acc = _dot(h_ref[...], w_ref[...])
    o_ref[...] = x_ref[...] + mod_ref[0][2:3] * acc


def _rwkv_out(geom, on, r, k, v, g, vec, w_o, x, mod):
    tm, tn = 256, 512
    R = geom.rows
    pair_spec = pl.BlockSpec((N_SLAB, tm, LANES), lambda i, j: (0, i, 0))
    dir_spec = pl.BlockSpec((2, N_SLAB, tm, LANES), lambda i, j: (0, 0, i, 0))
    return pl.pallas_call(
        _rwkv_out_kernel, grid=(R // tm, D_MODEL // tn),
        in_specs=[dir_spec, pair_spec, pair_spec, pair_spec, dir_spec,
                  _const_spec(vec.shape),
                  pl.BlockSpec((D_MODEL, tn), lambda i, j: (0, j)),
                  pl.BlockSpec((tm, tn), lambda i, j: (i, j)),
                  pl.BlockSpec((1, MOD_ROWS, tn), lambda i, j: (geom.seg(i, tm), 0, j))],
        out_specs=pl.BlockSpec((tm, tn), lambda i, j: (i, j)),
        out_shape=jax.ShapeDtypeStruct((R, D_MODEL), F32),
        scratch_shapes=[pltpu.VMEM((tm, D_MODEL), BF16)],
        compiler_params=_cparams(("arbitrary", "arbitrary")),
        name="rwkv_out",
    )(on, r, k, v, g, vec, w_o, x, mod)


def _attn_kernel(*refs, tq, tk, n_lat_chunks, lambda_init):
    if n_lat_chunks:
        (q_ref, kc_ref, vc_ref, kl_ref, vl_ref, lam_ref, sg_ref, o_ref,
         m_ref, l_ref, acc_ref, al_ref, sc_ref, p_ref, vtc_ref, vtl_ref) = refs
    else:
        (q_ref, kc_ref, vc_ref, lam_ref, sg_ref, o_ref,
         m_ref, l_ref, acc_ref, al_ref, sc_ref, p_ref, vtc_ref) = refs
    n_ctx = kc_ref.shape[0]

    @pl.when(pl.program_id(2) == 0)
    def _():
        vtc_ref[...] = vc_ref[...].astype(F32).T.astype(BF16)
        for i in range(n_lat_chunks):
            vtl_ref[i] = vl_ref[i * tk:(i + 1) * tk, :].astype(F32).T.astype(BF16)

    lane = lax.broadcasted_iota(jnp.int32, (tq, LANES), 1)
    q = q_ref[...]
    zero = jnp.zeros_like(q)
    qs = (jnp.where(lane < HEAD_DIM, q, zero), jnp.where(lane < HEAD_DIM, zero, q))

    m_ref[...] = jnp.full_like(m_ref, -jnp.inf)
    l_ref[...] = jnp.zeros_like(l_ref)
    acc_ref[...] = jnp.zeros_like(acc_ref)

    def scores(slot, k):
        nk = k.shape[0]
        for s in range(2):
            sc_ref[slot, s, 0:nk, :] = _dot_nt(k, qs[s])

    def consume(slot, nk, vt):
        for s in range(2):
            for t in range(tq // LANES):
                cols = slice(t * LANES, (t + 1) * LANES)
                x = sc_ref[slot, s, 0:nk, cols]
                m_old = m_ref[s, :, cols]
                m_new = jnp.maximum(m_old, jnp.max(x, axis=0, keepdims=True))
                alpha = jnp.exp2(m_old - m_new)
                pr = jnp.exp2(x - m_new)
                l_ref[s, :, cols] = alpha * l_ref[s, :, cols] + jnp.sum(pr, axis=0, keepdims=True)
                m_ref[s, :, cols] = m_new
                al_ref[s, :, cols] = alpha
                p_ref[s, 0:nk, cols] = pr.astype(BF16)
            acc_ref[s] = al_ref[s] * acc_ref[s] + _dot(vt, p_ref[s, 0:nk, :])

    scores(0, kc_ref[...])
    consume(0, n_ctx, vtc_ref[...])

    if n_lat_chunks:
        def kblk(i):
            return kl_ref[pl.ds(pl.multiple_of(i * tk, tk), tk), :]

        def vblk(i):
            return vtl_ref[i]

        scores(0, kblk(0))
        n_pairs = (n_lat_chunks - 1) // 2

        def body(i, carry):
            scores(1, kblk(2 * i + 1))
            consume(0, tk, vblk(2 * i))
            scores(0, kblk(2 * i + 2))
            consume(1, tk, vblk(2 * i + 1))
            return carry

        lax.fori_loop(0, n_pairs, body, 0)
        done = 2 * n_pairs
        if n_lat_chunks - done == 2:
            scores(1, kblk(done + 1))
            consume(0, tk, vblk(done))
            consume(1, tk, vblk(done + 1))
        else:
            consume(0, tk, vblk(done))

    lv = lam_ref[...]
    lam = (jnp.exp(jnp.sum(lv[0:1] * lv[1:2], axis=-1, keepdims=True))
           - jnp.exp(jnp.sum(lv[2:3] * lv[3:4], axis=-1, keepdims=True)) + lambda_init)
    ot = acc_ref[0] / l_ref[0] - lam * (acc_ref[1] / l_ref[1])
    ot = ot * lax.rsqrt(jnp.mean(ot * ot, axis=0, keepdims=True) + SUBLN_EPS)
    o_ref[...] = (ot.T * sg_ref[0:1] * (1.0 - lambda_init)).astype(o_ref.dtype)


def _attention(geom, qkv, lam_vec, subln_g, lambda_init, *, need_ctx):
    B, L, C = geom.B, geom.L, geom.C
    n_heads = D_MODEL // LANES
    ctx_row_blk = geom.n_lat // C
    kc_spec = pl.BlockSpec((C, LANES), lambda b, h, qi: (ctx_row_blk + b, n_heads + h))
    vc_spec = pl.BlockSpec((C, LANES), lambda b, h, qi: (ctx_row_blk + b, 2 * n_heads + h))
    small = [_const_spec(lam_vec.shape), _const_spec(subln_g.shape)]

    def call(name, tq, q_map, n_q, tk, n_lat_chunks, extra_specs, rows_out):
        kern = functools.partial(_attn_kernel, tq=tq, tk=tk, n_lat_chunks=n_lat_chunks,
                                 lambda_init=lambda_init)
        n_kv = 2 + len(extra_specs)
        return pl.pallas_call(
            kern, grid=(B, n_heads, n_q),
            in_specs=[pl.BlockSpec((tq, LANES), q_map), kc_spec, vc_spec] + extra_specs + small,
            out_specs=pl.BlockSpec((tq, LANES), lambda b, h, qi: (q_map(b, h, qi)[0] - rows_out[0], h)),
            out_shape=jax.ShapeDtypeStruct((rows_out[1], D_MODEL), BF16),
            scratch_shapes=[pltpu.VMEM((2, 1, tq), F32),
                            pltpu.VMEM((2, 1, tq), F32),
                            pltpu.VMEM((2, LANES, tq), F32),
                            pltpu.VMEM((2, 1, tq), F32),
                            pltpu.VMEM((2, 2, max(tk, C), tq), F32),
                            pltpu.VMEM((2, max(tk, C), tq), BF16),
                            pltpu.VMEM((LANES, C), BF16)]
            + ([pltpu.VMEM((n_lat_chunks, LANES, tk), BF16)] if n_lat_chunks else []),
            compiler_params=_cparams(("arbitrary", "arbitrary", "arbitrary")),
            name=name,
        )(*([qkv] * (1 + n_kv) + [lam_vec, subln_g]))

    tq = min(ATTN_TQ, L)
    tk = min(ATTN_TK, L)
    lat_specs = [pl.BlockSpec((L, LANES), lambda b, h, qi: (b, n_heads + h)),
                 pl.BlockSpec((L, LANES), lambda b, h, qi: (b, 2 * n_heads + h))]
    o_lat = call("diff_attention", tq, lambda b, h, qi: (b * (L // tq) + qi, h), L // tq, tk, L // tk,
                 lat_specs, (0, geom.n_lat))
    if not need_ctx:
        return o_lat
    o_ctx = call("diff_attention_ctx", C, lambda b, h, qi: (ctx_row_blk + b, h), 1, tk, 0, [],
                 (ctx_row_blk, geom.n_ctx))
    return jnp.concatenate([o_lat, o_ctx], axis=0)


def _rope_tables(geom):
    L = geom.L
    t = jnp.arange(L, dtype=jnp.int32)
    inv = ROPE_THETA ** (-jnp.arange(ROPE_PAIRS, dtype=F32) / ROPE_PAIRS)
    lane = jnp.arange(LANES, dtype=jnp.int32) % HEAD_DIM
    freq = inv[lane % ROPE_PAIRS]
    pos = jnp.where((lane < HEAD_DIM // 2)[None, :], (t // GRID_W)[:, None], (t % GRID_W)[:, None])
    ang = pos.astype(F32) * freq[None, :]
    sign = jnp.where((lane % 32) < 16, -1.0, 1.0)
    cos_l = jnp.cos(ang)
    sin_l = jnp.sin(ang) * sign[None, :]
    cos = jnp.concatenate([jnp.tile(cos_l, (geom.B, 1)), jnp.ones((geom.n_ctx, LANES), F32)], axis=0)
    sin = jnp.concatenate([jnp.tile(sin_l, (geom.B, 1)), jnp.zeros((geom.n_ctx, LANES), F32)], axis=0)
    return cos, sin


def _pad_cols(w, n):
    return jnp.pad(w, ((0, 0), (0, n - w.shape[1])))


def _pad_rows(w, n, offset=0):
    return jnp.pad(w, ((offset, n - offset - w.shape[0]), (0, 0)))


def _rwkv_weights(j, norm_g_row, rw_mix, rw_w_rkv, rw_w0, rw_w1, rw_w2, rw_a0, rw_a1, rw_a2, rw_g1,
                  rw_g2, rw_kk, rw_ka, rw_v0, rw_v1, rw_v2):
    lora_w = rw_w1.shape[-1]
    lora_g = rw_g1.shape[-1]
    g_pad = 3 * LANES
    wts = {
        "norm_g": norm_g_row.reshape(1, D_MODEL),
        "mix": jnp.pad(rw_mix[j], ((0, 2), (0, 0))),
        "wrkv": rw_w_rkv[j].astype(BF16),
        "w1": jnp.concatenate([rw_w1[j, 0], rw_w1[j, 1]], axis=1).astype(BF16),
        "w2f": _pad_rows(rw_w2[j, 0], LANES).astype(BF16),
        "w2b": _pad_rows(rw_w2[j, 1], LANES, lora_w).astype(BF16),
        "g1": _pad_cols(jnp.concatenate([rw_g1[j, 0], rw_g1[j, 1]], axis=1), g_pad).astype(BF16),
        "g2f": _pad_rows(rw_g2[j, 0], g_pad).astype(BF16),
        "g2b": _pad_rows(rw_g2[j, 1], g_pad, lora_g).astype(BF16),
        "a1": _pad_cols(rw_a1[j], LANES).astype(BF16),
        "a2": _pad_rows(rw_a2[j], LANES).astype(BF16),
    }
    v0 = rw_v0[j - 1] if j > 0 else jnp.zeros((D_MODEL,), F32)
    wts["vec"] = jnp.stack([rw_w0[j, 0], rw_w0[j, 1], rw_a0[j], rw_kk[j], rw_ka[j], v0,
                            jnp.zeros((D_MODEL,), F32), jnp.zeros((D_MODEL,), F32)], axis=0)
    if j > 0:
        wts["v1"] = _pad_cols(rw_v1[j - 1], LANES).astype(BF16)
        wts["v2"] = _pad_rows(rw_v2[j - 1], LANES).astype(BF16)
    return wts


def kernel(x, c, ctx, c_ctx, ada_w, ada_b, norm_g, final_g, rw_mix, rw_w_rkv, rw_w0, rw_w1, rw_w2, rw_a0, rw_a1, rw_a2, rw_g1, rw_g2, rw_kk, rw_ka, rw_rk, rw_ln_g, rw_ln_b, rw_w_o, rw_v0, rw_v1, rw_v2, da_w_qkv, da_w_o, da_lq1, da_lk1, da_lq2, da_lk2, da_subln_g, mlp_w1, mlp_w2):
    B, L, D = x.shape
    C = ctx.shape[1]
    depth = ada_w.shape[0]
    geom = _Geom(B, L, C)
    tm = geom.tm

    n_seg = 16
    cc = jnp.concatenate([c, c_ctx[None, :], jnp.zeros((n_seg - B - 1, D), F32)], axis=0)
    mod_all = _ada_table(cc, ada_w, ada_b).reshape(depth, n_seg, 6, D)
    mod_all = jnp.pad(mod_all, ((0, 0), (0, 0), (0, MOD_ROWS - 6), (0, 0)))

    xs = jnp.concatenate([x.reshape(B * L, D), ctx.reshape(B * C, D)], axis=0)
    rope_cos, rope_sin = _rope_tables(geom)
    vfirst = None

    for i in range(depth):
        last = i == depth - 1
        mod = mod_all[i]
        j = i // 2
        ng0 = norm_g[i, 0].reshape(1, D)
        ng1 = norm_g[i, 1].reshape(1, D)
        n_rows = geom.n_lat if last else geom.rows
        tn = 512
        xtile = pl.BlockSpec((tm, tn), lambda ti, tj: (ti, tj))
        if i % 2 == 0:
            wts = _rwkv_weights(j, norm_g[i, 0], rw_mix, rw_w_rkv, rw_w0, rw_w1, rw_w2, rw_a0, rw_a1,
                                rw_a2, rw_g1, rw_g2, rw_kk, rw_ka, rw_v0, rw_v1, rw_v2)
            r, k, v, kk, a, lw, g = _rwkv_features(geom, xs, mod, wts, vfirst)
            if j == 0:
                vfirst = v
            on = _wkv_scan(geom, r, k, v, kk, a, lw)
            vec = jnp.stack([rw_rk[j].reshape(D), rw_ln_g[j], rw_ln_b[j]]
                            + [jnp.zeros((D,), F32)] * 5, axis=0)
            xs = _rwkv_out(geom, on, r, k, v, g, vec, rw_w_o[j].astype(BF16), xs, mod)
        else:
            lambda_init = 0.8 - 0.6 * math.exp(-0.3 * i)
            rope_spec = pl.BlockSpec((tm, LANES), lambda ti, tj: (ti, 0))
            qkv = _mm_call("attn_qkv", xs, da_w_qkv[j].astype(BF16), n_rows=geom.rows, tm=tm, tn=tn,
                           prologue=_normmod_prologue(0, 1),
                           pro_inputs=[(mod, _mod_full_spec(geom, tm)), (ng0, _const_spec(ng0.shape))],
                           epilogue=_rope_epilogue(2 * D // tn, tn),
                           epi_inputs=[(rope_cos, rope_spec), (rope_sin, rope_spec)],
                           out_dtype=BF16)
            lam_vec = jnp.pad(jnp.stack([da_lq1[j], da_lk1[j], da_lq2[j], da_lk2[j]], axis=0),
                              ((0, 4), (0, LANES - HEAD_DIM)))
            sg = da_subln_g[j].reshape(1, LANES)
            o_att = _attention(geom, qkv, lam_vec, sg, lambda_init, need_ctx=not last)
            xs = _mm_call("attn_out", o_att, da_w_o[j].astype(BF16), n_rows=n_rows, tm=tm, tn=tn,
                          prologue=None, pro_inputs=[], epilogue=_gated_residual_epilogue(2),
                          epi_inputs=[(xs, xtile), (mod, _mod_tile_spec(geom, tm, tn))],
                          out_dtype=F32)
        hmid = _mm_call("mlp_up", xs, mlp_w1[i].astype(BF16), n_rows=n_rows, tm=tm, tn=tn,
                        prologue=_normmod_prologue(3, 4),
                        pro_inputs=[(mod, _mod_full_spec(geom, tm)), (ng1, _const_spec(ng1.shape))],
                        epilogue=_relu2_epilogue, epi_inputs=[], out_dtype=BF16)
        if last:
            tmf = min(tm, 512)
            fg = final_g.reshape(1, D)
            xs = _mm_call("mlp_down_final", hmid, mlp_w2[i].astype(BF16), n_rows=n_rows, tm=tmf, tn=D,
                          prologue=None, pro_inputs=[], epilogue=_gated_residual_norm_epilogue(5),
                          epi_inputs=[(xs, pl.BlockSpec((tmf, D), lambda ti, tj: (ti, 0))),
                                      (mod, _mod_tile_spec(geom, tmf, D)),
                                      (fg, _const_spec(fg.shape))],
                          out_dtype=F32)
        else:
            xs = _mm_call("mlp_down", hmid, mlp_w2[i].astype(BF16), n_rows=n_rows, tm=tm, tn=tn,
                          prologue=None, pro_inputs=[], epilogue=_gated_residual_epilogue(5),
                          epi_inputs=[(xs, xtile), (mod, _mod_tile_spec(geom, tm, tn))],
                          out_dtype=F32)
    return xs[:B * L].reshape(B, L, D)
```

```python
import functools
import math

import jax
import jax.numpy as jnp
from jax import lax
from jax.experimental import pallas as pl
from jax.experimental.pallas import tpu as pltpu

D_MODEL = 1024
HEAD_DIM = 64
LANES = 128
N_SLAB = D_MODEL // LANES
D_FF = 4 * D_MODEL
GRID_W = 64
ROPE_THETA = 10000.0
ROPE_PAIRS = HEAD_DIM // 4
NORM_EPS = 1e-6
SUBLN_EPS = 1e-5
GN_EPS = 64e-5
CHUNK = 64
SCAN_BLOCK = 256
ATTN_TQ = 512
ATTN_TK = 512
ATTN_ROWS = 64
MOD_ROWS = 8
VMEM_LIMIT = 56 * 1024 * 1024

F32 = jnp.float32
BF16 = jnp.bfloat16


def _cparams(sem):
    return pltpu.CompilerParams(dimension_semantics=sem, vmem_limit_bytes=VMEM_LIMIT)


def _const_spec(shape):
    nd = len(shape)
    return pl.BlockSpec(shape, lambda *_: (0,) * nd, pipeline_mode=pl.Buffered(1))


def _dot(a, b):
    return jnp.dot(a, b, preferred_element_type=F32)


def _dot_nt(a, b):
    return lax.dot_general(a, b, (((1,), (1,)), ((), ())), preferred_element_type=F32)


def _split2(x):
    hi = x.astype(BF16)
    lo = (x - hi.astype(F32)).astype(BF16)
    return hi, lo


def _split3(x):
    hi = x.astype(BF16)
    r1 = x - hi.astype(F32)
    mid = r1.astype(BF16)
    lo = (r1 - mid.astype(F32)).astype(BF16)
    return hi, mid, lo


def _rms_mod(x, g, shift, scale):
    y = x * lax.rsqrt(jnp.mean(x * x, axis=-1, keepdims=True) + NORM_EPS) * g
    return y * (1.0 + scale) + shift


def _head_sum_mat():
    r = lax.broadcasted_iota(jnp.int32, (LANES, LANES), 0) // HEAD_DIM
    c = lax.broadcasted_iota(jnp.int32, (LANES, LANES), 1) // HEAD_DIM
    return (r == c).astype(BF16)


def _head_sum(x, ones_bd2):
    hi, lo = _split2(x)
    return _dot(jnp.concatenate([hi, lo], axis=1), ones_bd2)


def _ada_kernel(cc_ref, w_ref, b_ref, o_ref):
    cc = cc_ref[...]
    s = cc * jax.nn.sigmoid(cc)
    o_ref[0] = jnp.dot(s, w_ref[0], preferred_element_type=F32,
                       precision=lax.Precision.HIGHEST) + b_ref[0]


def _ada_table(cc, ada_w, ada_b):
    depth, d, n = ada_w.shape
    rows = cc.shape[0]
    tn = 1536
    return pl.pallas_call(
        _ada_kernel,
        grid=(depth, n // tn),
        in_specs=[pl.BlockSpec((rows, d), lambda i, j: (0, 0)),
                  pl.BlockSpec((1, d, tn), lambda i, j: (i, 0, j)),
                  pl.BlockSpec((1, 1, tn), lambda i, j: (i, 0, j))],
        out_specs=pl.BlockSpec((1, rows, tn), lambda i, j: (i, 0, j)),
        out_shape=jax.ShapeDtypeStruct((depth, rows, n), F32),
        compiler_params=_cparams(("arbitrary", "arbitrary")),
        name="ada_table",
    )(cc, ada_w, ada_b.reshape(depth, 1, n))


class _Geom:
    def __init__(self, B, L, C):
        self.B, self.L, self.C = B, L, C
        self.n_lat = B * L
        self.n_ctx = B * C
        self.rows = self.n_lat + self.n_ctx
        tm = 1024
        while (L % tm) or (self.n_ctx % tm):
            tm //= 2
        self.tm = tm

    def seg(self, i, tm):
        r0 = i * tm
        return jnp.where(r0 < self.n_lat, r0 // self.L, self.B)


def _mm_kernel(*refs, n_pro, n_epi, prologue, epilogue):
    x_ref = refs[0]
    pro_refs = refs[1:1 + n_pro]
    w_ref = refs[1 + n_pro]
    epi_refs = refs[2 + n_pro:2 + n_pro + n_epi]
    o_ref = refs[2 + n_pro + n_epi]
    j = pl.program_id(1)
    if prologue is not None:
        h_ref = refs[3 + n_pro + n_epi]

        @pl.when(j == 0)
        def _():
            h_ref[...] = prologue(x_ref, pro_refs).astype(BF16)

        lhs = h_ref[...]
    else:
        lhs = x_ref[...]
    acc = _dot(lhs, w_ref[...])
    epilogue(o_ref, acc, epi_refs, j)


def _mm_call(name, x, w, *, n_rows, tm, tn, prologue, pro_inputs, epilogue, epi_inputs,
             out_dtype):
    K, N = w.shape
    in_specs = [pl.BlockSpec((tm, K), lambda i, j: (i, 0))]
    args = [x]
    for a, spec in pro_inputs:
        in_specs.append(spec)
        args.append(a)
    in_specs.append(pl.BlockSpec((K, tn), lambda i, j: (0, j)))
    args.append(w)
    for a, spec in epi_inputs:
        in_specs.append(spec)
        args.append(a)
    scratch = [pltpu.VMEM((tm, K), BF16)] if prologue is not None else []
    kern = functools.partial(_mm_kernel, n_pro=len(pro_inputs), n_epi=len(epi_inputs),
                             prologue=prologue, epilogue=epilogue)
    return pl.pallas_call(
        kern, grid=(n_rows // tm, N // tn), in_specs=in_specs,
        out_specs=pl.BlockSpec((tm, tn), lambda i, j: (i, j)),
        out_shape=jax.ShapeDtypeStruct((n_rows, N), out_dtype),
        scratch_shapes=scratch,
        compiler_params=_cparams(("arbitrary", "arbitrary")),
        name=name,
    )(*args)


def _mod_full_spec(geom, tm):
    return pl.BlockSpec((1, MOD_ROWS, D_MODEL), lambda i, j: (geom.seg(i, tm), 0, 0))


def _mod_tile_spec(geom, tm, tn):
    return pl.BlockSpec((1, MOD_ROWS, tn), lambda i, j: (geom.seg(i, tm), 0, j))


def _normmod_prologue(shift_row, scale_row):
    def prologue(x_ref, pro_refs):
        mod_ref, g_ref = pro_refs
        m = mod_ref[0]
        return _rms_mod(x_ref[...], g_ref[...], m[shift_row:shift_row + 1],
                        m[scale_row:scale_row + 1])
    return prologue


def _rope_epilogue(n_rope_tiles, tn):
    def epilogue(o_ref, acc, epi_refs, j):
        cos_ref, sin_ref = epi_refs

        @pl.when(j < n_rope_tiles)
        def _():
            f = jnp.where(j < n_rope_tiles // 2, HEAD_DIM ** -0.5 * math.log2(math.e), 1.0)
            cos = cos_ref[...] * f
            sin = sin_ref[...] * f
            lane = lax.broadcasted_iota(jnp.int32, cos.shape, 1)
            low = (lane & 16) == 0
            for s in range(tn // LANES):
                xs = acc[:, s * LANES:(s + 1) * LANES]
                partner = jnp.where(low, pltpu.roll(xs, LANES - 16, 1), pltpu.roll(xs, 16, 1))
                o_ref[:, s * LANES:(s + 1) * LANES] = (xs * cos + partner * sin).astype(o_ref.dtype)

        @pl.when(j >= n_rope_tiles)
        def _():
            o_ref[...] = acc.astype(o_ref.dtype)
    return epilogue


def _gated_residual_epilogue(gate_row):
    def epilogue(o_ref, acc, epi_refs, j):
        x_ref, mod_ref = epi_refs
        gate = mod_ref[0][gate_row:gate_row + 1]
        o_ref[...] = x_ref[...] + gate * acc
    return epilogue


def _gated_residual_norm_epilogue(gate_row):
    def epilogue(o_ref, acc, epi_refs, j):
        x_ref, mod_ref, g_ref = epi_refs
        gate = mod_ref[0][gate_row:gate_row + 1]
        x = x_ref[...] + gate * acc
        o_ref[...] = x * lax.rsqrt(jnp.mean(x * x, axis=-1, keepdims=True) + NORM_EPS) * g_ref[...]
    return epilogue


def _relu2_epilogue(o_ref, acc, epi_refs, j):
    a = jnp.maximum(acc, 0.0)
    o_ref[...] = (a * a).astype(o_ref.dtype)


def _rwkv_feat_kernel(*refs, tm, seg_lat, seg_ctx, n_lat, has_vres):
    (x_ref, xp_ref, xn_ref, mod_ref, ng_ref, mix_ref, wrkv_ref, w1_ref, w2f_ref, w2b_ref,
     g1_ref, g2f_ref, g2b_ref, a1_ref, a2_ref, vec_ref) = refs[:16]
    pos = 16
    if has_vres:
        v1_ref, v2_ref, vfirst_ref = refs[pos:pos + 3]
        pos += 3
    (r_o, k_o, v_o, kk_o, a_o, lw_o, g_o) = refs[pos:pos + 7]

    i = pl.program_id(0)
    m = mod_ref[0]
    shift, scale = m[0:1], m[1:2]
    g = ng_ref[...]
    h = _rms_mod(x_ref[...], g, shift, scale)
    hp = _rms_mod(xp_ref[...], g, shift, scale)[7:8]
    hn = _rms_mod(xn_ref[...], g, shift, scale)[0:1]

    row = lax.broadcasted_iota(jnp.int32, (tm, 1), 0)
    rid = row + i * tm
    in_lat = rid < n_lat
    seg_pos = jnp.where(in_lat, rid % seg_lat, (rid - n_lat) % seg_ctx)
    seg_len = jnp.where(in_lat, seg_lat, seg_ctx)
    prev = pltpu.roll(h, 1, 0)
    prev = jnp.where(row == 0, hp, prev)
    prev = jnp.where(seg_pos == 0, 0.0, prev)
    nxt = pltpu.roll(h, tm - 1, 0)
    nxt = jnp.where(row == tm - 1, hn, nxt)
    nxt = jnp.where(seg_pos == seg_len - 1, 0.0, nxt)
    xx = 0.5 * (prev + nxt) - h

    def mixed(mi):
        return (h + xx * mix_ref[mi:mi + 1]).astype(BF16)

    vec = vec_ref[...]
    ones_bd = _head_sum_mat()
    ones_bd2 = jnp.concatenate([ones_bd, ones_bd], axis=0)

    def store(o_ref, val):
        for p in range(N_SLAB):
            o_ref[p] = val[:, p * LANES:(p + 1) * LANES]

    xr = mixed(0)
    store(r_o, _dot(xr, wrkv_ref[0]))

    xw = mixed(1)
    tw = jnp.tanh(_dot(xw, w1_ref[...])).astype(BF16)
    neg_e = -math.exp(-0.5)
    store(lw_o.at[0], neg_e * jax.nn.sigmoid(vec[0:1] + _dot(tw, w2f_ref[...])))
    store(lw_o.at[1], neg_e * jax.nn.sigmoid(vec[1:2] + _dot(tw, w2b_ref[...])))

    xg = mixed(5)
    sg = jax.nn.sigmoid(_dot(xg, g1_ref[...])).astype(BF16)
    store(g_o.at[0], _dot(sg, g2f_ref[...]))
    store(g_o.at[1], _dot(sg, g2b_ref[...]))

    xa = mixed(4)
    a = jax.nn.sigmoid(vec[2:3] + _dot(_dot(xa, a1_ref[...]).astype(BF16), a2_ref[...]))
    store(a_o, a)

    xv = mixed(3)
    v = _dot(xv, wrkv_ref[2])
    if has_vres:
        gate = jax.nn.sigmoid(vec[5:6] + _dot(_dot(xv, v1_ref[...]).astype(BF16), v2_ref[...]))
        for p in range(N_SLAB):
            sl = slice(p * LANES, (p + 1) * LANES)
            vp = v[:, sl]
            v_o[p] = vp + (vfirst_ref[p] - vp) * gate[:, sl]
    else:
        store(v_o, v)

    xk = mixed(2)
    k = _dot(xk, wrkv_ref[1])
    k_scale = 1.0 + (a - 1.0) * vec[4:5]
    kraw = k * vec[3:4]
    for p in range(N_SLAB):
        sl = slice(p * LANES, (p + 1) * LANES)
        kr = kraw[:, sl]
        ss = _head_sum(kr * kr, ones_bd2)
        kk_o[p] = kr * lax.rsqrt(jnp.maximum(ss, 1e-24))
        k_o[p] = k[:, sl] * k_scale[:, sl]


def _rwkv_features(geom, x, mod, wts, vfirst):
    tm = 256
    R = geom.rows
    has_vres = vfirst is not None
    nblk8 = R // 8
    pm = functools.partial
    in_specs = [
        pl.BlockSpec((tm, D_MODEL), lambda i: (i, 0)),
        pl.BlockSpec((8, D_MODEL), lambda i: (jnp.maximum(i * (tm // 8) - 1, 0), 0)),
        pl.BlockSpec((8, D_MODEL), lambda i: (jnp.minimum((i + 1) * (tm // 8), nblk8 - 1), 0)),
        pl.BlockSpec((1, MOD_ROWS, D_MODEL), lambda i: (geom.seg(i, tm), 0, 0)),
    ]
    args = [x, x, x, mod]
    names = ["norm_g", "mix", "wrkv", "w1", "w2f", "w2b", "g1", "g2f", "g2b", "a1", "a2", "vec"]
    if has_vres:
        names += ["v1", "v2"]
    for nme in names:
        a = wts[nme]
        in_specs.append(_const_spec(a.shape))
        args.append(a)
    pair_spec = pl.BlockSpec((N_SLAB, tm, LANES), lambda i: (0, i, 0))
    dir_spec = pl.BlockSpec((2, N_SLAB, tm, LANES), lambda i: (0, 0, i, 0))
    if has_vres:
        in_specs.append(pair_spec)
        args.append(vfirst)
    pair_shape = jax.ShapeDtypeStruct((N_SLAB, R, LANES), F32)
    dir_shape = jax.ShapeDtypeStruct((2, N_SLAB, R, LANES), F32)
    kern = pm(_rwkv_feat_kernel, tm=tm, seg_lat=geom.L, seg_ctx=geom.C, n_lat=geom.n_lat,
              has_vres=has_vres)
    return pl.pallas_call(
        kern, grid=(R // tm,), in_specs=in_specs,
        out_specs=[pair_spec] * 5 + [dir_spec] * 2, out_shape=[pair_shape] * 5 + [dir_shape] * 2,
        compiler_params=_cparams(("arbitrary",)),
        name="rwkv_features",
    )(*args)


def _bd_rows(x, lane_lo):
    return jnp.concatenate([jnp.where(lane_lo, x, 0.0), jnp.where(lane_lo, 0.0, x)], axis=0)


def _wkv_kernel(*refs, n_chunk):
    in_refs = (refs[0:6], refs[6:12])
    o_refs = refs[12:14]
    st_ref = refs[14]
    j = pl.program_id(1)
    c = CHUNK

    @pl.when(j == 0)
    def _():
        st_ref[...] = jnp.zeros_like(st_ref)

    t_i = lax.broadcasted_iota(jnp.int32, (c, LANES), 0)
    lane = lax.broadcasted_iota(jnp.int32, (c, LANES), 1)
    s_i = lane % c
    lane_lo = lane < c
    eye = t_i == s_i
    blk = [(t_i // n) == (s_i // n) for n in (8, 16, 32)]
    tc = lax.broadcasted_iota(jnp.int32, (c, 3 * c), 0)
    sc = lax.broadcasted_iota(jnp.int32, (c, 3 * c), 1) % c
    row1 = lax.broadcasted_iota(jnp.int32, (c, 1), 0)
    strict_d = (s_i < t_i, s_i > t_i)
    incl_d = (s_i <= t_i, s_i >= t_i)
    mcum_d = ((sc <= tc).astype(BF16), (sc >= tc).astype(BF16))
    is_last_d = (row1 == c - 1, row1 == 0)
    bd_avg = _head_sum_mat() * (1.0 / HEAD_DIM)
    bd_avg2 = jnp.concatenate([bd_avg, bd_avg], axis=0)
    kr = lax.broadcasted_iota(jnp.int32, (LANES, LANES), 0)
    kc = lax.broadcasted_iota(jnp.int32, (LANES, LANES), 1)
    same_head = (kr // HEAD_DIM) == (kc // HEAD_DIM)
    diag = kr == kc
    eye_f = jnp.where(eye, 1.0, 0.0)

    def bf(x):
        return x.astype(BF16)

    def pmul(x_pair, y_pair):
        return _dot(bf(x_pair), bf(_bd_rows(y_pair, lane_lo)))

    def chunk_body(ci, carry):
        ds_d = (pl.ds(pl.multiple_of(ci * c, c), c),
                pl.ds(pl.multiple_of((n_chunk - 1 - ci) * c, c), c))
        chains = [(d, p) for d in range(2) for p in range(N_SLAB)]
        strict = [strict_d[d] for d, _ in chains]
        incl = [incl_d[d] for d, _ in chains]

        def each(fn, *lists):
            return [fn(*args) for args in zip(*lists)]

        def load(which):
            return [in_refs[d][which][0, p, ds_d[d], :] if which == 5
                    else in_refs[d][which][p, ds_d[d], :] for d, p in chains]

        lw = load(5)
        cum = [_dot(mcum_d[d], jnp.concatenate(_split3(x), axis=0))
               for (d, _), x in zip(chains, lw)]
        e_pos = each(jnp.exp, cum)
        e_neg = each(lambda x: jnp.exp(-x), cum)
        e_exc = each(lambda x, y: jnp.exp(x - y), cum, lw)
        p_end = [jnp.sum(jnp.where(is_last_d[d], e, 0.0), axis=0, keepdims=True)
                 for (d, _), e in zip(chains, e_pos)]
        r_t = each(lambda x, e: x * e, load(0), e_pos)
        kk = load(3)
        a_t = each(lambda x, e: -x * e, kk, e_exc)
        b_t = each(lambda x, y, e: x * y * e, kk, load(4), e_neg)
        k_t = each(lambda x, e: x * e, load(1), e_neg)
        v = load(2)
        gram = each(lambda at, rt, bt, kt: _dot_nt(
            bf(jnp.concatenate([at, rt], axis=0)),
            bf(jnp.concatenate([_bd_rows(bt, lane_lo), _bd_rows(kt, lane_lo)], axis=0))),
            a_t, r_t, b_t, k_t)
        l_ab = each(lambda g, m: jnp.where(m, g[:c, :LANES], 0.0), gram, strict)
        l_ak = each(lambda g, m: jnp.where(m, g[:c, LANES:], 0.0), gram, strict)
        p_rb = each(lambda g, m: jnp.where(m, g[c:, :LANES], 0.0), gram, incl)
        p_rk = each(lambda g, m: jnp.where(m, g[c:, LANES:], 0.0), gram, incl)

        n0 = each(lambda x: jnp.where(blk[0], x, 0.0), l_ab)
        p1 = each(pmul, n0, n0)
        t0 = each(lambda x: eye_f + x, n0)
        both = each(lambda t, q: pmul(jnp.concatenate([t, q], axis=0), q), t0, p1)
        t1 = each(lambda t, bo: t + bo[:c], t0, both)
        tinv = each(lambda t, bo: t + pmul(t, bo[c:]), t1, both)
        prev_blk = blk[0]
        for nb in (blk[1], blk[2], None):
            sel = jnp.logical_not(prev_blk) if nb is None else (nb & jnp.logical_not(prev_blk))
            mid = each(lambda x, t: pmul(jnp.where(sel, x, 0.0), t), l_ab, tinv)
            tinv = each(lambda t, m: t + pmul(t, m), tinv, mid)
            prev_blk = nb

        st = [st_ref[d, p] for d, p in chains]
        st_b = each(bf, st)
        v_bd = each(lambda x: bf(_bd_rows(x, lane_lo)), v)
        rhs_u = each(lambda at, lk, sb, vb: _dot(bf(jnp.concatenate([at, lk], axis=1)),
                                                 jnp.concatenate([sb, vb], axis=0)),
                     a_t, l_ak, st_b, v_bd)
        u_p = each(pmul, tinv, rhs_u)
        o = each(lambda rt, prb, prk, sb, up, vb: _dot(
            bf(jnp.concatenate([rt, prb, prk], axis=1)),
            jnp.concatenate([sb, bf(_bd_rows(up, lane_lo)), vb], axis=0)),
            r_t, p_rb, p_rk, st_b, u_p, v_bd)

        upd = each(lambda bt, kt, pe, up, vv: _dot(
            bf(jnp.concatenate([bt * pe, kt * pe], axis=0).T),
            bf(jnp.concatenate([up, vv], axis=0))),
            b_t, k_t, p_end, u_p, v)
        for i, (d, p) in enumerate(chains):
            p_col = jnp.sum(jnp.where(diag, p_end[i], 0.0), axis=1, keepdims=True)
            st_ref[d, p] = st[i] * p_col + jnp.where(same_head, upd[i], 0.0)

        mu = each(lambda x: _dot(jnp.concatenate(_split2(x), axis=1), bf(bd_avg2)), o)
        dlt = each(lambda x, m: x - m, o, mu)
        var = each(lambda x: _dot(jnp.concatenate(_split2(x * x), axis=1), bf(bd_avg2)), dlt)
        for i, (d, p) in enumerate(chains):
            o_refs[d][p, ds_d[d], :] = dlt[i] * lax.rsqrt(var[i] + GN_EPS)
        return carry

    lax.fori_loop(0, n_chunk, chunk_body, 0)


def _wkv_scan(geom, r, k, v, kk, a, lw):
    B, L, C = geom.B, geom.L, geom.C
    blk = SCAN_BLOCK
    n_lat_blk = L // blk
    n_ctx_blk = C // blk
    n_steps = n_lat_blk + n_ctx_blk
    lat_blocks = geom.n_lat // blk

    def row_block(d, b, j):
        in_ctx = j < n_ctx_blk
        jc = jnp.where(d == 0, j, n_ctx_blk - 1 - j)
        jl = jnp.where(d == 0, j - n_ctx_blk, n_steps - 1 - j)
        return jnp.where(in_ctx, lat_blocks + b * n_ctx_blk + jc, b * n_lat_blk + jl)

    def specs(d):
        pair = pl.BlockSpec((N_SLAB, blk, LANES), lambda b, j: (0, row_block(d, b, j), 0))
        lw_spec = pl.BlockSpec((1, N_SLAB, blk, LANES), lambda b, j: (d, 0, row_block(d, b, j), 0))
        return pair, lw_spec

    (pair_f, lw_f), (pair_b, lw_b) = specs(0), specs(1)
    kern = functools.partial(_wkv_kernel, n_chunk=blk // CHUNK)
    out_shape = jax.ShapeDtypeStruct((N_SLAB, geom.rows, LANES), F32)
    return pl.pallas_call(
        kern, grid=(B, n_steps),
        in_specs=[pair_f] * 5 + [lw_f] + [pair_b] * 5 + [lw_b],
        out_specs=[pair_f, pair_b],
        out_shape=[out_shape, out_shape],
        scratch_shapes=[pltpu.VMEM((2, N_SLAB, LANES, LANES), F32)],
        compiler_params=_cparams(("arbitrary", "arbitrary")),
        name="wkv_scan",
    )(r, k, v, kk, a, lw, r, k, v, kk, a, lw)


def _rwkv_out_kernel(onf_ref, onb_ref, r_ref, k_ref, v_ref, g_ref, vec_ref, w_ref, x_ref, mod_ref,
                     o_ref, h_ref):
    j = pl.program_id(1)

    @pl.when(j == 0)
    def _():
        ones_bd = _head_sum_mat()
        ones_bd2 = jnp.concatenate([ones_bd, ones_bd], axis=0)
        vec = vec_ref[...]
        for p in range(N_SLAB):
            sl = slice(p * LANES, (p + 1) * LANES)
            rk, lg, lb = vec[0:1, sl], vec[1:2, sl], vec[2:3, sl]
            bonus = _head_sum(r_ref[p] * k_ref[p] * rk, ones_bd2) * v_ref[p]
            y = ((onf_ref[p] * lg + lb + bonus) * g_ref[0, p]
                 + (onb_ref[p] * lg + lb + bonus) * g_ref[1, p])
            h_ref[:, sl] = y.astype(BF16)

    acc = _dot(h_ref[...], w_ref[...])
    o_ref[...] = x_ref[...] + mod_ref[0][2:3] * acc


def _rwkv_out(geom, on_f, on_b, r, k, v, g, vec, w_o, x, mod):
    tm, tn = 256, 512
    R = geom.rows
    pair_spec = pl.BlockSpec((N_SLAB, tm, LANES), lambda i, j: (0, i, 0))
    dir_spec = pl.BlockSpec((2, N_SLAB, tm, LANES), lambda i, j: (0, 0, i, 0))
    return pl.pallas_call(
        _rwkv_out_kernel, grid=(R // tm, D_MODEL // tn),
        in_specs=[pair_spec, pair_spec, pair_spec, pair_spec, pair_spec, dir_spec,
                  _const_spec(vec.shape),
                  pl.BlockSpec((D_MODEL, tn), lambda i, j: (0, j)),
                  pl.BlockSpec((tm, tn), lambda i, j: (i, j)),
                  pl.BlockSpec((1, MOD_ROWS, tn), lambda i, j: (geom.seg(i, tm), 0, j))],
        out_specs=pl.BlockSpec((tm, tn), lambda i, j: (i, j)),
        out_shape=jax.ShapeDtypeStruct((R, D_MODEL), F32),
        scratch_shapes=[pltpu.VMEM((tm, D_MODEL), BF16)],
        compiler_params=_cparams(("arbitrary", "arbitrary")),
        name="rwkv_out",
    )(on_f, on_b, r, k, v, g, vec, w_o, x, mod)


def _attn_kernel(*refs, tq, tk, n_lat_chunks, lambda_init):
    if n_lat_chunks:
        (q_ref, kc_ref, vc_ref, kl_ref, vl_ref, lam_ref, sg_ref, o_ref,
         m_ref, l_ref, acc_ref, al_ref, sc_ref, p_ref, vtc_ref, vtl_ref) = refs
    else:
        (q_ref, kc_ref, vc_ref, lam_ref, sg_ref, o_ref,
         m_ref, l_ref, acc_ref, al_ref, sc_ref, p_ref, vtc_ref) = refs
    n_ctx = kc_ref.shape[0]

    @pl.when(pl.program_id(2) == 0)
    def _():
        vtc_ref[...] = vc_ref[...].astype(F32).T.astype(BF16)
        for i in range(n_lat_chunks):
            vtl_ref[i] = vl_ref[i * tk:(i + 1) * tk, :].astype(F32).T.astype(BF16)

    lane = lax.broadcasted_iota(jnp.int32, (tq, LANES), 1)
    q = q_ref[...]
    zero = jnp.zeros_like(q)
    qs = (jnp.where(lane < HEAD_DIM, q, zero), jnp.where(lane < HEAD_DIM, zero, q))

    m_ref[...] = jnp.full_like(m_ref, -jnp.inf)
    l_ref[...] = jnp.zeros_like(l_ref)
    acc_ref[...] = jnp.zeros_like(acc_ref)

    n_qt = tq // LANES

    def scores(slot, k):
        nk = k.shape[0]
        for s in range(2):
            res = _dot_nt(k, qs[s])
            for t in range(n_qt):
                sc_ref[slot, s, t, 0:nk, :] = res[:, t * LANES:(t + 1) * LANES]

    def consume(slot, nk, vt):
        for s in range(2):
            for t in range(n_qt):
                cols = slice(t * LANES, (t + 1) * LANES)
                x = sc_ref[slot, s, t, 0:nk, :]
                m_old = m_ref[s, :, cols]
                m_new = jnp.maximum(m_old, jnp.max(x, axis=0, keepdims=True))
                alpha = jnp.exp2(m_old - m_new)
                pr = jnp.exp2(x - m_new)
                l_ref[s, :, cols] = alpha * l_ref[s, :, cols] + jnp.sum(pr, axis=0, keepdims=True)
                m_ref[s, :, cols] = m_new
                al_ref[s, :, cols] = alpha
                p_ref[s, t, 0:nk, :] = pr.astype(BF16)
            pt = jnp.concatenate([p_ref[s, t, 0:nk, :] for t in range(n_qt)], axis=1)
            acc_ref[s] = al_ref[s] * acc_ref[s] + _dot(vt, pt)

    scores(0, kc_ref[...])
    consume(0, n_ctx, vtc_ref[...])

    if n_lat_chunks:
        def kblk(i):
            return kl_ref[pl.ds(pl.multiple_of(i * tk, tk), tk), :]

        def vblk(i):
            return vtl_ref[i]

        scores(0, kblk(0))
        n_pairs = (n_lat_chunks - 1) // 2

        def body(i, carry):
            scores(1, kblk(2 * i + 1))
            consume(0, tk, vblk(2 * i))
            scores(0, kblk(2 * i + 2))
            consume(1, tk, vblk(2 * i + 1))
            return carry

        lax.fori_loop(0, n_pairs, body, 0)
        done = 2 * n_pairs
        if n_lat_chunks - done == 2:
            scores(1, kblk(done + 1))
            consume(0, tk, vblk(done))
            consume(1, tk, vblk(done + 1))
        else:
            consume(0, tk, vblk(done))

    lv = lam_ref[...]
    lam = (jnp.exp(jnp.sum(lv[0:1] * lv[1:2], axis=-1, keepdims=True))
           - jnp.exp(jnp.sum(lv[2:3] * lv[3:4], axis=-1, keepdims=True)) + lambda_init)
    ot = acc_ref[0] / l_ref[0] - lam * (acc_ref[1] / l_ref[1])
    ot = ot * lax.rsqrt(jnp.mean(ot * ot, axis=0, keepdims=True) + SUBLN_EPS)
    o_ref[...] = (ot.T * sg_ref[0:1] * (1.0 - lambda_init)).astype(o_ref.dtype)


def _attention(geom, qkv, lam_vec, subln_g, lambda_init, *, need_ctx):
    B, L, C = geom.B, geom.L, geom.C
    n_heads = D_MODEL // LANES
    ctx_row_blk = geom.n_lat // C
    kc_spec = pl.BlockSpec((C, LANES), lambda b, h, qi: (ctx_row_blk + b, n_heads + h))
    vc_spec = pl.BlockSpec((C, LANES), lambda b, h, qi: (ctx_row_blk + b, 2 * n_heads + h))
    small = [_const_spec(lam_vec.shape), _const_spec(subln_g.shape)]

    def call(name, tq, q_map, n_q, tk, n_lat_chunks, extra_specs, rows_out):
        kern = functools.partial(_attn_kernel, tq=tq, tk=tk, n_lat_chunks=n_lat_chunks,
                                 lambda_init=lambda_init)
        n_kv = 2 + len(extra_specs)
        return pl.pallas_call(
            kern, grid=(B, n_heads, n_q),
            in_specs=[pl.BlockSpec((tq, LANES), q_map), kc_spec, vc_spec] + extra_specs + small,
            out_specs=pl.BlockSpec((tq, LANES), lambda b, h, qi: (q_map(b, h, qi)[0] - rows_out[0], h)),
            out_shape=jax.ShapeDtypeStruct((rows_out[1], D_MODEL), BF16),
            scratch_shapes=[pltpu.VMEM((2, 1, tq), F32),
                            pltpu.VMEM((2, 1, tq), F32),
                            pltpu.VMEM((2, LANES, tq), F32),
                            pltpu.VMEM((2, 1, tq), F32),
                            pltpu.VMEM((2, 2, tq // LANES, max(tk, C), LANES), F32),
                            pltpu.VMEM((2, tq // LANES, max(tk, C), LANES), BF16),
                            pltpu.VMEM((LANES, C), BF16)]
            + ([pltpu.VMEM((n_lat_chunks, LANES, tk), BF16)] if n_lat_chunks else []),
            compiler_params=_cparams(("arbitrary", "arbitrary", "arbitrary")),
            name=name,
        )(*([qkv] * (1 + n_kv) + [lam_vec, subln_g]))

    tq = min(ATTN_TQ, L)
    tk = min(ATTN_TK, L)
    lat_specs = [pl.BlockSpec((L, LANES), lambda b, h, qi: (b, n_heads + h)),
                 pl.BlockSpec((L, LANES), lambda b, h, qi: (b, 2 * n_heads + h))]
    o_lat = call("diff_attention", tq, lambda b, h, qi: (b * (L // tq) + qi, h), L // tq, tk, L // tk,
                 lat_specs, (0, geom.n_lat))
    if not need_ctx:
        return o_lat
    o_ctx = call("diff_attention_ctx", C, lambda b, h, qi: (ctx_row_blk + b, h), 1, tk, 0, [],
                 (ctx_row_blk, geom.n_ctx))
    return jnp.concatenate([o_lat, o_ctx], axis=0)


def _rope_tables(geom):
    L = geom.L
    t = jnp.arange(L, dtype=jnp.int32)
    inv = ROPE_THETA ** (-jnp.arange(ROPE_PAIRS, dtype=F32) / ROPE_PAIRS)
    lane = jnp.arange(LANES, dtype=jnp.int32) % HEAD_DIM
    freq = inv[lane % ROPE_PAIRS]
    pos = jnp.where((lane < HEAD_DIM // 2)[None, :], (t // GRID_W)[:, None], (t % GRID_W)[:, None])
    ang = pos.astype(F32) * freq[None, :]
    sign = jnp.where((lane % 32) < 16, -1.0, 1.0)
    cos_l = jnp.cos(ang)
    sin_l = jnp.sin(ang) * sign[None, :]
    cos = jnp.concatenate([jnp.tile(cos_l, (geom.B, 1)), jnp.ones((geom.n_ctx, LANES), F32)], axis=0)
    sin = jnp.concatenate([jnp.tile(sin_l, (geom.B, 1)), jnp.zeros((geom.n_ctx, LANES), F32)], axis=0)
    return cos, sin


def _pad_cols(w, n):
    return jnp.pad(w, ((0, 0), (0, n - w.shape[1])))


def _pad_rows(w, n, offset=0):
    return jnp.pad(w, ((offset, n - offset - w.shape[0]), (0, 0)))


def _rwkv_weights(j, norm_g_row, rw_mix, rw_w_rkv, rw_w0, rw_w1, rw_w2, rw_a0, rw_a1, rw_a2, rw_g1,
                  rw_g2, rw_kk, rw_ka, rw_v0, rw_v1, rw_v2):
    lora_w = rw_w1.shape[-1]
    lora_g = rw_g1.shape[-1]
    g_pad = 3 * LANES
    wts = {
        "norm_g": norm_g_row.reshape(1, D_MODEL),
        "mix": jnp.pad(rw_mix[j], ((0, 2), (0, 0))),
        "wrkv": rw_w_rkv[j].astype(BF16),
        "w1": jnp.concatenate([rw_w1[j, 0], rw_w1[j, 1]], axis=1).astype(BF16),
        "w2f": _pad_rows(rw_w2[j, 0], LANES).astype(BF16),
        "w2b": _pad_rows(rw_w2[j, 1], LANES, lora_w).astype(BF16),
        "g1": _pad_cols(jnp.concatenate([rw_g1[j, 0], rw_g1[j, 1]], axis=1), g_pad).astype(BF16),
        "g2f": _pad_rows(rw_g2[j, 0], g_pad).astype(BF16),
        "g2b": _pad_rows(rw_g2[j, 1], g_pad, lora_g).astype(BF16),
        "a1": _pad_cols(rw_a1[j], LANES).astype(BF16),
        "a2": _pad_rows(rw_a2[j], LANES).astype(BF16),
    }
    v0 = rw_v0[j - 1] if j > 0 else jnp.zeros((D_MODEL,), F32)
    wts["vec"] = jnp.stack([rw_w0[j, 0], rw_w0[j, 1], rw_a0[j], rw_kk[j], rw_ka[j], v0,
                            jnp.zeros((D_MODEL,), F32), jnp.zeros((D_MODEL,), F32)], axis=0)
    if j > 0:
        wts["v1"] = _pad_cols(rw_v1[j - 1], LANES).astype(BF16)
        wts["v2"] = _pad_rows(rw_v2[j - 1], LANES).astype(BF16)
    return wts


def kernel(x, c, ctx, c_ctx, ada_w, ada_b, norm_g, final_g, rw_mix, rw_w_rkv, rw_w0, rw_w1, rw_w2, rw_a0, rw_a1, rw_a2, rw_g1, rw_g2, rw_kk, rw_ka, rw_rk, rw_ln_g, rw_ln_b, rw_w_o, rw_v0, rw_v1, rw_v2, da_w_qkv, da_w_o, da_lq1, da_lk1, da_lq2, da_lk2, da_subln_g, mlp_w1, mlp_w2):
    B, L, D = x.shape
    C = ctx.shape[1]
    depth = ada_w.shape[0]
    geom = _Geom(B, L, C)
    tm = geom.tm

    n_seg = 16
    cc = jnp.concatenate([c, c_ctx[None, :], jnp.zeros((n_seg - B - 1, D), F32)], axis=0)
    mod_all = _ada_table(cc, ada_w, ada_b).reshape(depth, n_seg, 6, D)
    mod_all = jnp.pad(mod_all, ((0, 0), (0, 0), (0, MOD_ROWS - 6), (0, 0)))

    xs = jnp.concatenate([x.reshape(B * L, D), ctx.reshape(B * C, D)], axis=0)
    rope_cos, rope_sin = _rope_tables(geom)
    vfirst = None

    for i in range(depth):
        last = i == depth - 1
        mod = mod_all[i]
        j = i // 2
        ng0 = norm_g[i, 0].reshape(1, D)
        ng1 = norm_g[i, 1].reshape(1, D)
        n_rows = geom.n_lat if last else geom.rows
        tn = 512
        xtile = pl.BlockSpec((tm, tn), lambda ti, tj: (ti, tj))
        if i % 2 == 0:
            wts = _rwkv_weights(j, norm_g[i, 0], rw_mix, rw_w_rkv, rw_w0, rw_w1, rw_w2, rw_a0, rw_a1,
                                rw_a2, rw_g1, rw_g2, rw_kk, rw_ka, rw_v0, rw_v1, rw_v2)
            r, k, v, kk, a, lw, g = _rwkv_features(geom, xs, mod, wts, vfirst)
            if j == 0:
                vfirst = v
            on_f, on_b = _wkv_scan(geom, r, k, v, kk, a, lw)
            vec = jnp.stack([rw_rk[j].reshape(D), rw_ln_g[j], rw_ln_b[j]]
                            + [jnp.zeros((D,), F32)] * 5, axis=0)
            xs = _rwkv_out(geom, on_f, on_b, r, k, v, g, vec, rw_w_o[j].astype(BF16), xs, mod)
        else:
            lambda_init = 0.8 - 0.6 * math.exp(-0.3 * i)
            rope_spec = pl.BlockSpec((tm, LANES), lambda ti, tj: (ti, 0))
            qkv = _mm_call("attn_qkv", xs, da_w_qkv[j].astype(BF16), n_rows=geom.rows, tm=tm, tn=tn,
                           prologue=_normmod_prologue(0, 1),
                           pro_inputs=[(mod, _mod_full_spec(geom, tm)), (ng0, _const_spec(ng0.shape))],
                           epilogue=_rope_epilogue(2 * D // tn, tn),
                           epi_inputs=[(rope_cos, rope_spec), (rope_sin, rope_spec)],
                           out_dtype=BF16)
            lam_vec = jnp.pad(jnp.stack([da_lq1[j], da_lk1[j], da_lq2[j], da_lk2[j]], axis=0),
                              ((0, 4), (0, LANES - HEAD_DIM)))
            sg = da_subln_g[j].reshape(1, LANES)
            o_att = _attention(geom, qkv, lam_vec, sg, lambda_init, need_ctx=not last)
            xs = _mm_call("attn_out", o_att, da_w_o[j].astype(BF16), n_rows=n_rows, tm=tm, tn=tn,
                          prologue=None, pro_inputs=[], epilogue=_gated_residual_epilogue(2),
                          epi_inputs=[(xs, xtile), (mod, _mod_tile_spec(geom, tm, tn))],
                          out_dtype=F32)
        hmid = _mm_call("mlp_up", xs, mlp_w1[i].astype(BF16), n_rows=n_rows, tm=tm, tn=2 * tn,
                        prologue=_normmod_prologue(3, 4),
                        pro_inputs=[(mod, _mod_full_spec(geom, tm)), (ng1, _const_spec(ng1.shape))],
                        epilogue=_relu2_epilogue, epi_inputs=[], out_dtype=BF16)
        if last:
            tmf = min(tm, 512)
            fg = final_g.reshape(1, D)
            xs = _mm_call("mlp_down_final", hmid, mlp_w2[i].astype(BF16), n_rows=n_rows, tm=tmf, tn=D,
                          prologue=None, pro_inputs=[], epilogue=_gated_residual_norm_epilogue(5),
                          epi_inputs=[(xs, pl.BlockSpec((tmf, D), lambda ti, tj: (ti, 0))),
                                      (mod, _mod_tile_spec(geom, tmf, D)),
                                      (fg, _const_spec(fg.shape))],
                          out_dtype=F32)
        else:
            xs = _mm_call("mlp_down", hmid, mlp_w2[i].astype(BF16), n_rows=n_rows, tm=tm, tn=tn,
                          prologue=None, pro_inputs=[], epilogue=_gated_residual_epilogue(5),
                          epi_inputs=[(xs, xtile), (mod, _mod_tile_spec(geom, tm, tn))],
                          out_dtype=F32)
    return xs[:B * L].reshape(B, L, D)
```

```python
import functools
import math

import jax
import jax.numpy as jnp
from jax import lax
from jax.experimental import pallas as pl
from jax.experimental.pallas import tpu as pltpu

D_MODEL = 1024
HEAD_DIM = 64
LANES = 128
N_SLAB = D_MODEL // LANES
D_FF = 4 * D_MODEL
GRID_W = 64
ROPE_THETA = 10000.0
ROPE_PAIRS = HEAD_DIM // 4
NORM_EPS = 1e-6
SUBLN_EPS = 1e-5
GN_EPS = 64e-5
CHUNK = 64
SCAN_BLOCK = 256
ATTN_TQ = 1024
ATTN_TK = 512
MOD_ROWS = 8
VMEM_LIMIT = 56 * 1024 * 1024

F32 = jnp.float32
BF16 = jnp.bfloat16


def _cparams(sem):
    return pltpu.CompilerParams(dimension_semantics=sem, vmem_limit_bytes=VMEM_LIMIT)


def _const_spec(shape):
    nd = len(shape)
    return pl.BlockSpec(shape, lambda *_: (0,) * nd, pipeline_mode=pl.Buffered(1))


def _dot(a, b):
    return jnp.dot(a, b, preferred_element_type=F32)


def _dot_nt(a, b):
    return lax.dot_general(a, b, (((1,), (1,)), ((), ())), preferred_element_type=F32)


def _split2(x):
    hi = x.astype(BF16)
    lo = (x - hi.astype(F32)).astype(BF16)
    return hi, lo


def _split3(x):
    hi = x.astype(BF16)
    r1 = x - hi.astype(F32)
    mid = r1.astype(BF16)
    lo = (r1 - mid.astype(F32)).astype(BF16)
    return hi, mid, lo


def _rms_mod(x, g, shift, scale):
    y = x * lax.rsqrt(jnp.mean(x * x, axis=-1, keepdims=True) + NORM_EPS) * g
    return y * (1.0 + scale) + shift


def _head_sum_mat():
    r = lax.broadcasted_iota(jnp.int32, (LANES, LANES), 0) // HEAD_DIM
    c = lax.broadcasted_iota(jnp.int32, (LANES, LANES), 1) // HEAD_DIM
    return (r == c).astype(BF16)


def _head_sum(x, ones_bd2):
    hi, lo = _split2(x)
    return _dot(jnp.concatenate([hi, lo], axis=1), ones_bd2)


def _ada_kernel(cc_ref, w_ref, b_ref, o_ref):
    cc = cc_ref[...]
    s = cc * jax.nn.sigmoid(cc)
    o_ref[0] = jnp.dot(s, w_ref[0], preferred_element_type=F32,
                       precision=lax.Precision.HIGHEST) + b_ref[0]


def _ada_table(cc, ada_w, ada_b):
    depth, d, n = ada_w.shape
    rows = cc.shape[0]
    tn = 1536
    return pl.pallas_call(
        _ada_kernel,
        grid=(depth, n // tn),
        in_specs=[pl.BlockSpec((rows, d), lambda i, j: (0, 0)),
                  pl.BlockSpec((1, d, tn), lambda i, j: (i, 0, j)),
                  pl.BlockSpec((1, 1, tn), lambda i, j: (i, 0, j))],
        out_specs=pl.BlockSpec((1, rows, tn), lambda i, j: (i, 0, j)),
        out_shape=jax.ShapeDtypeStruct((depth, rows, n), F32),
        compiler_params=_cparams(("arbitrary", "arbitrary")),
        name="ada_table",
    )(cc, ada_w, ada_b.reshape(depth, 1, n))


class _Geom:
    def __init__(self, B, L, C):
        self.B, self.L, self.C = B, L, C
        self.n_lat = B * L
        self.n_ctx = B * C
        self.rows = self.n_lat + self.n_ctx
        tm = 1024
        while (L % tm) or (self.n_ctx % tm):
            tm //= 2
        self.tm = tm

    def seg(self, i, tm):
        r0 = i * tm
        return jnp.where(r0 < self.n_lat, r0 // self.L, self.B)


def _mm_kernel(*refs, n_pro, n_epi, prologue, epilogue):
    x_ref = refs[0]
    pro_refs = refs[1:1 + n_pro]
    w_ref = refs[1 + n_pro]
    epi_refs = refs[2 + n_pro:2 + n_pro + n_epi]
    o_ref = refs[2 + n_pro + n_epi]
    j = pl.program_id(1)
    if prologue is not None:
        h_ref = refs[3 + n_pro + n_epi]

        @pl.when(j == 0)
        def _():
            h_ref[...] = prologue(x_ref, pro_refs).astype(BF16)

        lhs = h_ref[...]
    else:
        lhs = x_ref[...]
    acc = _dot(lhs, w_ref[...])
    epilogue(o_ref, acc, epi_refs, j)


def _mm_call(name, x, w, *, n_rows, tm, tn, prologue, pro_inputs, epilogue, epi_inputs,
             out_dtype):
    K, N = w.shape
    in_specs = [pl.BlockSpec((tm, K), lambda i, j: (i, 0))]
    args = [x]
    for a, spec in pro_inputs:
        in_specs.append(spec)
        args.append(a)
    in_specs.append(pl.BlockSpec((K, tn), lambda i, j: (0, j)))
    args.append(w)
    for a, spec in epi_inputs:
        in_specs.append(spec)
        args.append(a)
    scratch = [pltpu.VMEM((tm, K), BF16)] if prologue is not None else []
    kern = functools.partial(_mm_kernel, n_pro=len(pro_inputs), n_epi=len(epi_inputs),
                             prologue=prologue, epilogue=epilogue)
    return pl.pallas_call(
        kern, grid=(n_rows // tm, N // tn), in_specs=in_specs,
        out_specs=pl.BlockSpec((tm, tn), lambda i, j: (i, j)),
        out_shape=jax.ShapeDtypeStruct((n_rows, N), out_dtype),
        scratch_shapes=scratch,
        compiler_params=_cparams(("arbitrary", "arbitrary")),
        name=name,
    )(*args)


def _mod_full_spec(geom, tm):
    return pl.BlockSpec((1, MOD_ROWS, D_MODEL), lambda i, j: (geom.seg(i, tm), 0, 0))


def _mod_tile_spec(geom, tm, tn):
    return pl.BlockSpec((1, MOD_ROWS, tn), lambda i, j: (geom.seg(i, tm), 0, j))


def _normmod_prologue(shift_row, scale_row):
    def prologue(x_ref, pro_refs):
        mod_ref, g_ref = pro_refs
        m = mod_ref[0]
        return _rms_mod(x_ref[...], g_ref[...], m[shift_row:shift_row + 1],
                        m[scale_row:scale_row + 1])
    return prologue


def _rope_epilogue(n_rope_tiles, tn):
    def epilogue(o_ref, acc, epi_refs, j):
        cos_ref, sin_ref = epi_refs

        @pl.when(j < n_rope_tiles)
        def _():
            f = jnp.where(j < n_rope_tiles // 2, HEAD_DIM ** -0.5 * math.log2(math.e), 1.0)
            cos = cos_ref[...] * f
            sin = sin_ref[...] * f
            lane = lax.broadcasted_iota(jnp.int32, cos.shape, 1)
            low = (lane & 16) == 0
            for s in range(tn // LANES):
                xs = acc[:, s * LANES:(s + 1) * LANES]
                partner = jnp.where(low, pltpu.roll(xs, LANES - 16, 1), pltpu.roll(xs, 16, 1))
                o_ref[:, s * LANES:(s + 1) * LANES] = (xs * cos + partner * sin).astype(o_ref.dtype)

        @pl.when(j >= n_rope_tiles)
        def _():
            o_ref[...] = acc.astype(o_ref.dtype)
    return epilogue


def _gated_residual_epilogue(gate_row):
    def epilogue(o_ref, acc, epi_refs, j):
        x_ref, mod_ref = epi_refs
        gate = mod_ref[0][gate_row:gate_row + 1]
        o_ref[...] = x_ref[...] + gate * acc
    return epilogue


def _gated_residual_norm_epilogue(gate_row):
    def epilogue(o_ref, acc, epi_refs, j):
        x_ref, mod_ref, g_ref = epi_refs
        gate = mod_ref[0][gate_row:gate_row + 1]
        x = x_ref[...] + gate * acc
        o_ref[...] = x * lax.rsqrt(jnp.mean(x * x, axis=-1, keepdims=True) + NORM_EPS) * g_ref[...]
    return epilogue


def _relu2_epilogue(o_ref, acc, epi_refs, j):
    a = jnp.maximum(acc, 0.0)
    o_ref[...] = (a * a).astype(o_ref.dtype)


def _rwkv_feat_kernel(*refs, tm, seg_lat, seg_ctx, n_lat, has_vres):
    (x_ref, xp_ref, xn_ref, mod_ref, ng_ref, mix_ref, wrkv_ref, w1_ref, w2f_ref, w2b_ref,
     g1_ref, g2f_ref, g2b_ref, a1_ref, a2_ref, vec_ref) = refs[:16]
    pos = 16
    if has_vres:
        v1_ref, v2_ref, vfirst_ref = refs[pos:pos + 3]
        pos += 3
    (r_o, k_o, v_o, kk_o, a_o, lw_o, g_o) = refs[pos:pos + 7]

    i = pl.program_id(0)
    m = mod_ref[0]
    shift, scale = m[0:1], m[1:2]
    g = ng_ref[...]
    h = _rms_mod(x_ref[...], g, shift, scale)
    hp = _rms_mod(xp_ref[...], g, shift, scale)[7:8]
    hn = _rms_mod(xn_ref[...], g, shift, scale)[0:1]

    row = lax.broadcasted_iota(jnp.int32, (tm, 1), 0)
    rid = row + i * tm
    in_lat = rid < n_lat
    seg_pos = jnp.where(in_lat, rid % seg_lat, (rid - n_lat) % seg_ctx)
    seg_len = jnp.where(in_lat, seg_lat, seg_ctx)
    prev = pltpu.roll(h, 1, 0)
    prev = jnp.where(row == 0, hp, prev)
    prev = jnp.where(seg_pos == 0, 0.0, prev)
    nxt = pltpu.roll(h, tm - 1, 0)
    nxt = jnp.where(row == tm - 1, hn, nxt)
    nxt = jnp.where(seg_pos == seg_len - 1, 0.0, nxt)
    xx = 0.5 * (prev + nxt) - h

    def mixed(mi):
        return (h + xx * mix_ref[mi:mi + 1]).astype(BF16)

    vec = vec_ref[...]
    ones_bd = _head_sum_mat()
    ones_bd2 = jnp.concatenate([ones_bd, ones_bd], axis=0)

    def store(o_ref, val):
        for p in range(N_SLAB):
            o_ref[p] = val[:, p * LANES:(p + 1) * LANES].astype(o_ref.dtype)

    xr = mixed(0)
    store(r_o, _dot(xr, wrkv_ref[0]))

    xw = mixed(1)
    tw = jnp.tanh(_dot(xw, w1_ref[...])).astype(BF16)
    neg_e = -math.exp(-0.5)
    store(lw_o.at[0], neg_e * jax.nn.sigmoid(vec[0:1] + _dot(tw, w2f_ref[...])))
    store(lw_o.at[1], neg_e * jax.nn.sigmoid(vec[1:2] + _dot(tw, w2b_ref[...])))

    xg = mixed(5)
    sg = jax.nn.sigmoid(_dot(xg, g1_ref[...])).astype(BF16)
    store(g_o.at[0], _dot(sg, g2f_ref[...]))
    store(g_o.at[1], _dot(sg, g2b_ref[...]))

    xa = mixed(4)
    a = jax.nn.sigmoid(vec[2:3] + _dot(_dot(xa, a1_ref[...]).astype(BF16), a2_ref[...]))
    store(a_o, a)

    xv = mixed(3)
    v = _dot(xv, wrkv_ref[2])
    if has_vres:
        gate = jax.nn.sigmoid(vec[5:6] + _dot(_dot(xv, v1_ref[...]).astype(BF16), v2_ref[...]))
        for p in range(N_SLAB):
            sl = slice(p * LANES, (p + 1) * LANES)
            vp = v[:, sl]
            v_o[p] = (vp + (vfirst_ref[p].astype(F32) - vp) * gate[:, sl]).astype(v_o.dtype)
    else:
        store(v_o, v)

    xk = mixed(2)
    k = _dot(xk, wrkv_ref[1])
    k_scale = 1.0 + (a - 1.0) * vec[4:5]
    kraw = k * vec[3:4]
    for p in range(N_SLAB):
        sl = slice(p * LANES, (p + 1) * LANES)
        kr = kraw[:, sl]
        ss = _head_sum(kr * kr, ones_bd2)
        kk_o[p] = (kr * lax.rsqrt(jnp.maximum(ss, 1e-24))).astype(kk_o.dtype)
        k_o[p] = (k[:, sl] * k_scale[:, sl]).astype(k_o.dtype)


def _rwkv_features(geom, x, mod, wts, vfirst):
    tm = 256
    R = geom.rows
    has_vres = vfirst is not None
    nblk8 = R // 8
    pm = functools.partial
    in_specs = [
        pl.BlockSpec((tm, D_MODEL), lambda i: (i, 0)),
        pl.BlockSpec((8, D_MODEL), lambda i: (jnp.maximum(i * (tm // 8) - 1, 0), 0)),
        pl.BlockSpec((8, D_MODEL), lambda i: (jnp.minimum((i + 1) * (tm // 8), nblk8 - 1), 0)),
        pl.BlockSpec((1, MOD_ROWS, D_MODEL), lambda i: (geom.seg(i, tm), 0, 0)),
    ]
    args = [x, x, x, mod]
    names = ["norm_g", "mix", "wrkv", "w1", "w2f", "w2b", "g1", "g2f", "g2b", "a1", "a2", "vec"]
    if has_vres:
        names += ["v1", "v2"]
    for nme in names:
        a = wts[nme]
        in_specs.append(_const_spec(a.shape))
        args.append(a)
    pair_spec = pl.BlockSpec((N_SLAB, tm, LANES), lambda i: (0, i, 0))
    dir_spec = pl.BlockSpec((2, N_SLAB, tm, LANES), lambda i: (0, 0, i, 0))
    if has_vres:
        in_specs.append(pair_spec)
        args.append(vfirst)
    pair_shape = jax.ShapeDtypeStruct((N_SLAB, R, LANES), BF16)
    lw_shape = jax.ShapeDtypeStruct((2, N_SLAB, R, LANES), F32)
    g_shape = jax.ShapeDtypeStruct((2, N_SLAB, R, LANES), BF16)
    kern = pm(_rwkv_feat_kernel, tm=tm, seg_lat=geom.L, seg_ctx=geom.C, n_lat=geom.n_lat,
              has_vres=has_vres)
    return pl.pallas_call(
        kern, grid=(R // tm,), in_specs=in_specs,
        out_specs=[pair_spec] * 5 + [dir_spec] * 2, out_shape=[pair_shape] * 5 + [lw_shape, g_shape],
        compiler_params=_cparams(("arbitrary",)),
        name="rwkv_features",
    )(*args)


def _bd_rows(x, lane_lo):
    return jnp.concatenate([jnp.where(lane_lo, x, 0.0), jnp.where(lane_lo, 0.0, x)], axis=0)


def _wkv_kernel(*refs, n_chunk):
    in_refs = (refs[0:6], refs[6:12])
    o_refs = refs[12:14]
    st_ref = refs[14]
    j = pl.program_id(1)
    c = CHUNK

    @pl.when(j == 0)
    def _():
        st_ref[...] = jnp.zeros_like(st_ref)

    t_i = lax.broadcasted_iota(jnp.int32, (c, LANES), 0)
    lane = lax.broadcasted_iota(jnp.int32, (c, LANES), 1)
    s_i = lane % c
    lane_lo = lane < c
    eye = t_i == s_i
    blk = [(t_i // n) == (s_i // n) for n in (8, 16, 32)]
    tc = lax.broadcasted_iota(jnp.int32, (c, 3 * c), 0)
    sc = lax.broadcasted_iota(jnp.int32, (c, 3 * c), 1) % c
    row1 = lax.broadcasted_iota(jnp.int32, (c, 1), 0)
    strict_d = (s_i < t_i, s_i > t_i)
    incl_d = (s_i <= t_i, s_i >= t_i)
    mcum_d = ((sc <= tc).astype(BF16), (sc >= tc).astype(BF16))
    is_last_d = (row1 == c - 1, row1 == 0)
    bd_avg = _head_sum_mat() * (1.0 / HEAD_DIM)
    bd_avg2 = jnp.concatenate([bd_avg, bd_avg], axis=0)
    kr = lax.broadcasted_iota(jnp.int32, (LANES, LANES), 0)
    kc = lax.broadcasted_iota(jnp.int32, (LANES, LANES), 1)
    same_head = (kr // HEAD_DIM) == (kc // HEAD_DIM)
    diag = kr == kc
    eye_f = jnp.where(eye, 1.0, 0.0)

    def bf(x):
        return x.astype(BF16)

    def pmul(x_pair, y_pair):
        return _dot(bf(x_pair), bf(_bd_rows(y_pair, lane_lo)))

    def chunk_body(ci, carry):
        ds_d = (pl.ds(pl.multiple_of(ci * c, c), c),
                pl.ds(pl.multiple_of((n_chunk - 1 - ci) * c, c), c))
        chains = [(d, p) for d in range(2) for p in range(N_SLAB)]
        strict = [strict_d[d] for d, _ in chains]
        incl = [incl_d[d] for d, _ in chains]

        def each(fn, *lists):
            return [fn(*args) for args in zip(*lists)]

        def load(which):
            return [in_refs[d][which][0, p, ds_d[d], :] if which == 5
                    else in_refs[d][which][p, ds_d[d], :].astype(F32) for d, p in chains]

        lw = load(5)
        cum = [_dot(mcum_d[d], jnp.concatenate(_split3(x), axis=0))
               for (d, _), x in zip(chains, lw)]
        e_pos = each(jnp.exp, cum)
        e_neg = each(lambda x: jnp.exp(-x), cum)
        e_exc = each(lambda x, y: jnp.exp(x - y), cum, lw)
        p_end = [jnp.sum(jnp.where(is_last_d[d], e, 0.0), axis=0, keepdims=True)
                 for (d, _), e in zip(chains, e_pos)]
        r_t = each(lambda x, e: x * e, load(0), e_pos)
        kk = load(3)
        a_t = each(lambda x, e: -x * e, kk, e_exc)
        b_t = each(lambda x, y, e: x * y * e, kk, load(4), e_neg)
        k_t = each(lambda x, e: x * e, load(1), e_neg)
        v = load(2)
        gram = each(lambda at, rt, bt, kt: _dot_nt(
            bf(jnp.concatenate([at, rt], axis=0)),
            bf(jnp.concatenate([_bd_rows(bt, lane_lo), _bd_rows(kt, lane_lo)], axis=0))),
            a_t, r_t, b_t, k_t)
        l_ab = each(lambda g, m: jnp.where(m, g[:c, :LANES], 0.0), gram, strict)
        l_ak = each(lambda g, m: jnp.where(m, g[:c, LANES:], 0.0), gram, strict)
        p_rb = each(lambda g, m: jnp.where(m, g[c:, :LANES], 0.0), gram, incl)
        p_rk = each(lambda g, m: jnp.where(m, g[c:, LANES:], 0.0), gram, incl)

        n0 = each(lambda x: jnp.where(blk[0], x, 0.0), l_ab)
        p1 = each(pmul, n0, n0)
        t0 = each(lambda x: eye_f + x, n0)
        both = each(lambda t, q: pmul(jnp.concatenate([t, q], axis=0), q), t0, p1)
        t1 = each(lambda t, bo: t + bo[:c], t0, both)
        tinv = each(lambda t, bo: t + pmul(t, bo[c:]), t1, both)
        prev_blk = blk[0]
        for nb in (blk[1], blk[2], None):
            sel = jnp.logical_not(prev_blk) if nb is None else (nb & jnp.logical_not(prev_blk))
            mid = each(lambda x, t: pmul(jnp.where(sel, x, 0.0), t), l_ab, tinv)
            tinv = each(lambda t, m: t + pmul(t, m), tinv, mid)
            prev_blk = nb

        st = [st_ref[d, p] for d, p in chains]
        st_b = each(bf, st)
        v_bd = each(lambda x: bf(_bd_rows(x, lane_lo)), v)
        rhs_u = each(lambda at, lk, sb, vb: _dot(bf(jnp.concatenate([at, lk], axis=1)),
                                                 jnp.concatenate([sb, vb], axis=0)),
                     a_t, l_ak, st_b, v_bd)
        u_p = each(pmul, tinv, rhs_u)
        o = each(lambda rt, prb, prk, sb, up, vb: _dot(
            bf(jnp.concatenate([rt, prb, prk], axis=1)),
            jnp.concatenate([sb, bf(_bd_rows(up, lane_lo)), vb], axis=0)),
            r_t, p_rb, p_rk, st_b, u_p, v_bd)

        upd = each(lambda bt, kt, pe, up, vv: _dot(
            bf(jnp.concatenate([bt * pe, kt * pe], axis=0).T),
            bf(jnp.concatenate([up, vv], axis=0))),
            b_t, k_t, p_end, u_p, v)
        for i, (d, p) in enumerate(chains):
            p_col = jnp.sum(jnp.where(diag, p_end[i], 0.0), axis=1, keepdims=True)
            st_ref[d, p] = st[i] * p_col + jnp.where(same_head, upd[i], 0.0)

        mu = each(lambda x: _dot(jnp.concatenate(_split2(x), axis=1), bf(bd_avg2)), o)
        dlt = each(lambda x, m: x - m, o, mu)
        var = each(lambda x: _dot(jnp.concatenate(_split2(x * x), axis=1), bf(bd_avg2)), dlt)
        for i, (d, p) in enumerate(chains):
            o_refs[d][p, ds_d[d], :] = (dlt[i] * lax.rsqrt(var[i] + GN_EPS)).astype(o_refs[d].dtype)
        return carry

    lax.fori_loop(0, n_chunk, chunk_body, 0)


def _wkv_scan(geom, r, k, v, kk, a, lw):
    B, L, C = geom.B, geom.L, geom.C
    blk = SCAN_BLOCK
    n_lat_blk = L // blk
    n_ctx_blk = C // blk
    n_steps = n_lat_blk + n_ctx_blk
    lat_blocks = geom.n_lat // blk

    def row_block(d, b, j):
        in_ctx = j < n_ctx_blk
        jc = jnp.where(d == 0, j, n_ctx_blk - 1 - j)
        jl = jnp.where(d == 0, j - n_ctx_blk, n_steps - 1 - j)
        return jnp.where(in_ctx, lat_blocks + b * n_ctx_blk + jc, b * n_lat_blk + jl)

    def specs(d):
        pair = pl.BlockSpec((N_SLAB, blk, LANES), lambda b, j: (0, row_block(d, b, j), 0))
        lw_spec = pl.BlockSpec((1, N_SLAB, blk, LANES), lambda b, j: (d, 0, row_block(d, b, j), 0))
        return pair, lw_spec

    (pair_f, lw_f), (pair_b, lw_b) = specs(0), specs(1)
    kern = functools.partial(_wkv_kernel, n_chunk=blk // CHUNK)
    out_shape = jax.ShapeDtypeStruct((N_SLAB, geom.rows, LANES), BF16)
    return pl.pallas_call(
        kern, grid=(B, n_steps),
        in_specs=[pair_f] * 5 + [lw_f] + [pair_b] * 5 + [lw_b],
        out_specs=[pair_f, pair_b],
        out_shape=[out_shape, out_shape],
        scratch_shapes=[pltpu.VMEM((2, N_SLAB, LANES, LANES), F32)],
        compiler_params=_cparams(("arbitrary", "arbitrary")),
        name="wkv_scan",
    )(r, k, v, kk, a, lw, r, k, v, kk, a, lw)


def _rwkv_out_kernel(onf_ref, onb_ref, r_ref, k_ref, v_ref, g_ref, vec_ref, w_ref, x_ref, mod_ref,
                     o_ref, h_ref):
    j = pl.program_id(1)

    @pl.when(j == 0)
    def _():
        ones_bd = _head_sum_mat()
        ones_bd2 = jnp.concatenate([ones_bd, ones_bd], axis=0)
        vec = vec_ref[...]
        for p in range(N_SLAB):
            sl = slice(p * LANES, (p + 1) * LANES)
            rk, lg, lb = vec[0:1, sl], vec[1:2, sl], vec[2:3, sl]
            f32 = lambda ref, *idx: ref[idx].astype(F32)
            bonus = _head_sum(f32(r_ref, p) * f32(k_ref, p) * rk, ones_bd2) * f32(v_ref, p)
            y = ((f32(onf_ref, p) * lg + lb + bonus) * f32(g_ref, 0, p)
                 + (f32(onb_ref, p) * lg + lb + bonus) * f32(g_ref, 1, p))
            h_ref[:, sl] = y.astype(BF16)

    acc = _dot(h_ref[...], w_ref[...])
    o_ref[...] = x_ref[...] + mod_ref[0][2:3] * acc


def _rwkv_out(geom, on_f, on_b, r, k, v, g, vec, w_o, x, mod):
    tm, tn = 512, 512
    R = geom.rows
    pair_spec = pl.BlockSpec((N_SLAB, tm, LANES), lambda i, j: (0, i, 0))
    dir_spec = pl.BlockSpec((2, N_SLAB, tm, LANES), lambda i, j: (0, 0, i, 0))
    return pl.pallas_call(
        _rwkv_out_kernel, grid=(R // tm, D_MODEL // tn),
        in_specs=[pair_spec, pair_spec, pair_spec, pair_spec, pair_spec, dir_spec,
                  _const_spec(vec.shape),
                  pl.BlockSpec((D_MODEL, tn), lambda i, j: (0, j)),
                  pl.BlockSpec((tm, tn), lambda i, j: (i, j)),
                  pl.BlockSpec((1, MOD_ROWS, tn), lambda i, j: (geom.seg(i, tm), 0, j))],
        out_specs=pl.BlockSpec((tm, tn), lambda i, j: (i, j)),
        out_shape=jax.ShapeDtypeStruct((R, D_MODEL), F32),
        scratch_shapes=[pltpu.VMEM((tm, D_MODEL), BF16)],
        compiler_params=_cparams(("arbitrary", "arbitrary")),
        name="rwkv_out",
    )(on_f, on_b, r, k, v, g, vec, w_o, x, mod)


def _attn_kernel(*refs, tq, tk, n_lat_chunks, lambda_init):
    if n_lat_chunks:
        (q_ref, kc_ref, vc_ref, kl_ref, vl_ref, lam_ref, sg_ref, o_ref,
         m_ref, l_ref, acc_ref, al_ref, sc_ref, p_ref, vtc_ref, vtl_ref) = refs
    else:
        (q_ref, kc_ref, vc_ref, lam_ref, sg_ref, o_ref,
         m_ref, l_ref, acc_ref, al_ref, sc_ref, p_ref, vtc_ref) = refs
    n_ctx = kc_ref.shape[0]

    @pl.when(pl.program_id(2) == 0)
    def _():
        vtc_ref[...] = vc_ref[...].astype(F32).T.astype(BF16)
        for i in range(n_lat_chunks):
            vtl_ref[i] = vl_ref[i * tk:(i + 1) * tk, :].astype(F32).T.astype(BF16)

    lane = lax.broadcasted_iota(jnp.int32, (tq, LANES), 1)
    q = q_ref[...]
    zero = jnp.zeros_like(q)
    qs = (jnp.where(lane < HEAD_DIM, q, zero), jnp.where(lane < HEAD_DIM, zero, q))

    m_ref[...] = jnp.full_like(m_ref, -jnp.inf)
    l_ref[...] = jnp.zeros_like(l_ref)
    acc_ref[...] = jnp.zeros_like(acc_ref)

    n_qt = tq // LANES

    def scores(slot, k):
        nk = k.shape[0]
        for s in range(2):
            res = _dot_nt(k, qs[s])
            for t in range(n_qt):
                sc_ref[slot, s, t, 0:nk, :] = res[:, t * LANES:(t + 1) * LANES]

    def consume(slot, nk, vt):
        for s in range(2):
            for t in range(n_qt):
                cols = slice(t * LANES, (t + 1) * LANES)
                x = sc_ref[slot, s, t, 0:nk, :]
                m_old = m_ref[s, :, cols]
                m_new = jnp.maximum(m_old, jnp.max(x, axis=0, keepdims=True))
                alpha = jnp.exp2(m_old - m_new)
                pr = jnp.exp2(x - m_new)
                l_ref[s, :, cols] = alpha * l_ref[s, :, cols] + jnp.sum(pr, axis=0, keepdims=True)
                m_ref[s, :, cols] = m_new
                al_ref[s, :, cols] = alpha
                p_ref[s, t, 0:nk, :] = pr.astype(BF16)
            pt = jnp.concatenate([p_ref[s, t, 0:nk, :] for t in range(n_qt)], axis=1)
            acc_ref[s] = al_ref[s] * acc_ref[s] + _dot(vt, pt)

    scores(0, kc_ref[...])
    consume(0, n_ctx, vtc_ref[...])

    if n_lat_chunks:
        def kblk(i):
            return kl_ref[pl.ds(pl.multiple_of(i * tk, tk), tk), :]

        def vblk(i):
            return vtl_ref[i]

        scores(0, kblk(0))
        n_pairs = (n_lat_chunks - 1) // 2

        def body(i, carry):
            scores(1, kblk(2 * i + 1))
            consume(0, tk, vblk(2 * i))
            scores(0, kblk(2 * i + 2))
            consume(1, tk, vblk(2 * i + 1))
            return carry

        lax.fori_loop(0, n_pairs, body, 0)
        done = 2 * n_pairs
        if n_lat_chunks - done == 2:
            scores(1, kblk(done + 1))
            consume(0, tk, vblk(done))
            consume(1, tk, vblk(done + 1))
        else:
            consume(0, tk, vblk(done))

    lv = lam_ref[...]
    lam = (jnp.exp(jnp.sum(lv[0:1] * lv[1:2], axis=-1, keepdims=True))
           - jnp.exp(jnp.sum(lv[2:3] * lv[3:4], axis=-1, keepdims=True)) + lambda_init)
    ot = acc_ref[0] / l_ref[0] - lam * (acc_ref[1] / l_ref[1])
    ot = ot * lax.rsqrt(jnp.mean(ot * ot, axis=0, keepdims=True) + SUBLN_EPS)
    o_ref[...] = (ot.T * sg_ref[0:1] * (1.0 - lambda_init)).astype(o_ref.dtype)


def _attention(geom, qkv, lam_vec, subln_g, lambda_init, *, need_ctx):
    B, L, C = geom.B, geom.L, geom.C
    n_heads = D_MODEL // LANES
    ctx_row_blk = geom.n_lat // C
    kc_spec = pl.BlockSpec((C, LANES), lambda b, h, qi: (ctx_row_blk + b, n_heads + h))
    vc_spec = pl.BlockSpec((C, LANES), lambda b, h, qi: (ctx_row_blk + b, 2 * n_heads + h))
    small = [_const_spec(lam_vec.shape), _const_spec(subln_g.shape)]

    def call(name, tq, q_map, n_q, tk, n_lat_chunks, extra_specs, rows_out):
        kern = functools.partial(_attn_kernel, tq=tq, tk=tk, n_lat_chunks=n_lat_chunks,
                                 lambda_init=lambda_init)
        n_kv = 2 + len(extra_specs)
        return pl.pallas_call(
            kern, grid=(B, n_heads, n_q),
            in_specs=[pl.BlockSpec((tq, LANES), q_map), kc_spec, vc_spec] + extra_specs + small,
            out_specs=pl.BlockSpec((tq, LANES), lambda b, h, qi: (q_map(b, h, qi)[0] - rows_out[0], h)),
            out_shape=jax.ShapeDtypeStruct((rows_out[1], D_MODEL), BF16),
            scratch_shapes=[pltpu.VMEM((2, 1, tq), F32),
                            pltpu.VMEM((2, 1, tq), F32),
                            pltpu.VMEM((2, LANES, tq), F32),
                            pltpu.VMEM((2, 1, tq), F32),
                            pltpu.VMEM((2, 2, tq // LANES, max(tk, C), LANES), F32),
                            pltpu.VMEM((2, tq // LANES, max(tk, C), LANES), BF16),
                            pltpu.VMEM((LANES, C), BF16)]
            + ([pltpu.VMEM((n_lat_chunks, LANES, tk), BF16)] if n_lat_chunks else []),
            compiler_params=_cparams(("arbitrary", "arbitrary", "arbitrary")),
            name=name,
        )(*([qkv] * (1 + n_kv) + [lam_vec, subln_g]))

    tq = min(ATTN_TQ, L)
    tk = min(ATTN_TK, L)
    lat_specs = [pl.BlockSpec((L, LANES), lambda b, h, qi: (b, n_heads + h)),
                 pl.BlockSpec((L, LANES), lambda b, h, qi: (b, 2 * n_heads + h))]
    o_lat = call("diff_attention", tq, lambda b, h, qi: (b * (L // tq) + qi, h), L // tq, tk, L // tk,
                 lat_specs, (0, geom.n_lat))
    if not need_ctx:
        return o_lat
    o_ctx = call("diff_attention_ctx", C, lambda b, h, qi: (ctx_row_blk + b, h), 1, tk, 0, [],
                 (ctx_row_blk, geom.n_ctx))
    return jnp.concatenate([o_lat, o_ctx], axis=0)


def _rope_tables(geom):
    L = geom.L
    t = jnp.arange(L, dtype=jnp.int32)
    inv = ROPE_THETA ** (-jnp.arange(ROPE_PAIRS, dtype=F32) / ROPE_PAIRS)
    lane = jnp.arange(LANES, dtype=jnp.int32) % HEAD_DIM
    freq = inv[lane % ROPE_PAIRS]
    pos = jnp.where((lane < HEAD_DIM // 2)[None, :], (t // GRID_W)[:, None], (t % GRID_W)[:, None])
    ang = pos.astype(F32) * freq[None, :]
    sign = jnp.where((lane % 32) < 16, -1.0, 1.0)
    cos_l = jnp.cos(ang)
    sin_l = jnp.sin(ang) * sign[None, :]
    cos = jnp.concatenate([jnp.tile(cos_l, (geom.B, 1)), jnp.ones((geom.n_ctx, LANES), F32)], axis=0)
    sin = jnp.concatenate([jnp.tile(sin_l, (geom.B, 1)), jnp.zeros((geom.n_ctx, LANES), F32)], axis=0)
    return cos, sin


def _pad_cols(w, n):
    return jnp.pad(w, ((0, 0), (0, n - w.shape[1])))


def _pad_rows(w, n, offset=0):
    return jnp.pad(w, ((offset, n - offset - w.shape[0]), (0, 0)))


def _rwkv_weights(j, norm_g_row, rw_mix, rw_w_rkv, rw_w0, rw_w1, rw_w2, rw_a0, rw_a1, rw_a2, rw_g1,
                  rw_g2, rw_kk, rw_ka, rw_v0, rw_v1, rw_v2):
    lora_w = rw_w1.shape[-1]
    lora_g = rw_g1.shape[-1]
    g_pad = 3 * LANES
    wts = {
        "norm_g": norm_g_row.reshape(1, D_MODEL),
        "mix": jnp.pad(rw_mix[j], ((0, 2), (0, 0))),
        "wrkv": rw_w_rkv[j].astype(BF16),
        "w1": jnp.concatenate([rw_w1[j, 0], rw_w1[j, 1]], axis=1).astype(BF16),
        "w2f": _pad_rows(rw_w2[j, 0], LANES).astype(BF16),
        "w2b": _pad_rows(rw_w2[j, 1], LANES, lora_w).astype(BF16),
        "g1": _pad_cols(jnp.concatenate([rw_g1[j, 0], rw_g1[j, 1]], axis=1), g_pad).astype(BF16),
        "g2f": _pad_rows(rw_g2[j, 0], g_pad).astype(BF16),
        "g2b": _pad_rows(rw_g2[j, 1], g_pad, lora_g).astype(BF16),
        "a1": _pad_cols(rw_a1[j], LANES).astype(BF16),
        "a2": _pad_rows(rw_a2[j], LANES).astype(BF16),
    }
    v0 = rw_v0[j - 1] if j > 0 else jnp.zeros((D_MODEL,), F32)
    wts["vec"] = jnp.stack([rw_w0[j, 0], rw_w0[j, 1], rw_a0[j], rw_kk[j], rw_ka[j], v0,
                            jnp.zeros((D_MODEL,), F32), jnp.zeros((D_MODEL,), F32)], axis=0)
    if j > 0:
        wts["v1"] = _pad_cols(rw_v1[j - 1], LANES).astype(BF16)
        wts["v2"] = _pad_rows(rw_v2[j - 1], LANES).astype(BF16)
    return wts


def kernel(x, c, ctx, c_ctx, ada_w, ada_b, norm_g, final_g, rw_mix, rw_w_rkv, rw_w0, rw_w1, rw_w2, rw_a0, rw_a1, rw_a2, rw_g1, rw_g2, rw_kk, rw_ka, rw_rk, rw_ln_g, rw_ln_b, rw_w_o, rw_v0, rw_v1, rw_v2, da_w_qkv, da_w_o, da_lq1, da_lk1, da_lq2, da_lk2, da_subln_g, mlp_w1, mlp_w2):
    B, L, D = x.shape
    C = ctx.shape[1]
    depth = ada_w.shape[0]
    geom = _Geom(B, L, C)
    tm = geom.tm

    n_seg = 16
    cc = jnp.concatenate([c, c_ctx[None, :], jnp.zeros((n_seg - B - 1, D), F32)], axis=0)
    mod_all = _ada_table(cc, ada_w, ada_b).reshape(depth, n_seg, 6, D)
    mod_all = jnp.pad(mod_all, ((0, 0), (0, 0), (0, MOD_ROWS - 6), (0, 0)))

    xs = jnp.concatenate([x.reshape(B * L, D), ctx.reshape(B * C, D)], axis=0)
    rope_cos, rope_sin = _rope_tables(geom)
    vfirst = None

    for i in range(depth):
        last = i == depth - 1
        mod = mod_all[i]
        j = i // 2
        ng0 = norm_g[i, 0].reshape(1, D)
        ng1 = norm_g[i, 1].reshape(1, D)
        n_rows = geom.n_lat if last else geom.rows
        tn = 512
        xtile = pl.BlockSpec((tm, tn), lambda ti, tj: (ti, tj))
        if i % 2 == 0:
            wts = _rwkv_weights(j, norm_g[i, 0], rw_mix, rw_w_rkv, rw_w0, rw_w1, rw_w2, rw_a0, rw_a1,
                                rw_a2, rw_g1, rw_g2, rw_kk, rw_ka, rw_v0, rw_v1, rw_v2)
            r, k, v, kk, a, lw, g = _rwkv_features(geom, xs, mod, wts, vfirst)
            if j == 0:
                vfirst = v
            on_f, on_b = _wkv_scan(geom, r, k, v, kk, a, lw)
            vec = jnp.stack([rw_rk[j].reshape(D), rw_ln_g[j], rw_ln_b[j]]
                            + [jnp.zeros((D,), F32)] * 5, axis=0)
            xs = _rwkv_out(geom, on_f, on_b, r, k, v, g, vec, rw_w_o[j].astype(BF16), xs, mod)
        else:
            lambda_init = 0.8 - 0.6 * math.exp(-0.3 * i)
            rope_spec = pl.BlockSpec((tm, LANES), lambda ti, tj: (ti, 0))
            qkv = _mm_call("attn_qkv", xs, da_w_qkv[j].astype(BF16), n_rows=geom.rows, tm=tm, tn=tn,
                           prologue=_normmod_prologue(0, 1),
                           pro_inputs=[(mod, _mod_full_spec(geom, tm)), (ng0, _const_spec(ng0.shape))],
                           epilogue=_rope_epilogue(2 * D // tn, tn),
                           epi_inputs=[(rope_cos, rope_spec), (rope_sin, rope_spec)],
                           out_dtype=BF16)
            lam_vec = jnp.pad(jnp.stack([da_lq1[j], da_lk1[j], da_lq2[j], da_lk2[j]], axis=0),
                              ((0, 4), (0, LANES - HEAD_DIM)))
            sg = da_subln_g[j].reshape(1, LANES)
            o_att = _attention(geom, qkv, lam_vec, sg, lambda_init, need_ctx=not last)
            xs = _mm_call("attn_out", o_att, da_w_o[j].astype(BF16), n_rows=n_rows, tm=tm, tn=tn,
                          prologue=None, pro_inputs=[], epilogue=_gated_residual_epilogue(2),
                          epi_inputs=[(xs, xtile), (mod, _mod_tile_spec(geom, tm, tn))],
                          out_dtype=F32)
        hmid = _mm_call("mlp_up", xs, mlp_w1[i].astype(BF16), n_rows=n_rows, tm=tm, tn=2 * tn,
                        prologue=_normmod_prologue(3, 4),
                        pro_inputs=[(mod, _mod_full_spec(geom, tm)), (ng1, _const_spec(ng1.shape))],
                        epilogue=_relu2_epilogue, epi_inputs=[], out_dtype=BF16)
        if last:
            tmf = min(tm, 512)
            fg = final_g.reshape(1, D)
            xs = _mm_call("mlp_down_final", hmid, mlp_w2[i].astype(BF16), n_rows=n_rows, tm=tmf, tn=D,
                          prologue=None, pro_inputs=[], epilogue=_gated_residual_norm_epilogue(5),
                          epi_inputs=[(xs, pl.BlockSpec((tmf, D), lambda ti, tj: (ti, 0))),
                                      (mod, _mod_tile_spec(geom, tmf, D)),
                                      (fg, _const_spec(fg.shape))],
                          out_dtype=F32)
        else:
            xs = _mm_call("mlp_down", hmid, mlp_w2[i].astype(BF16), n_rows=n_rows, tm=tm, tn=tn,
                          prologue=None, pro_inputs=[], epilogue=_gated_residual_epilogue(5),
                          epi_inputs=[(xs, xtile), (mod, _mod_tile_spec(geom, tm, tn))],
                          out_dtype=F32)
    return xs[:B * L].reshape(B, L, D)
```

```python
import functools
import math

import jax
import jax.numpy as jnp
from jax import lax
from jax.experimental import pallas as pl
from jax.experimental.pallas import tpu as pltpu

D_MODEL = 1024
HEAD_DIM = 64
LANES = 128
N_SLAB = D_MODEL // LANES
D_FF = 4 * D_MODEL
GRID_W = 64
ROPE_THETA = 10000.0
ROPE_PAIRS = HEAD_DIM // 4
NORM_EPS = 1e-6
SUBLN_EPS = 1e-5
GN_EPS = 64e-5
CHUNK = 64
SCAN_BLOCK = 256
ATTN_TQ = 1024
ATTN_TK = 512
MOD_ROWS = 8
VMEM_LIMIT = 56 * 1024 * 1024

F32 = jnp.float32
BF16 = jnp.bfloat16


def _cparams(sem):
    return pltpu.CompilerParams(dimension_semantics=sem, vmem_limit_bytes=VMEM_LIMIT)


def _const_spec(shape):
    nd = len(shape)
    return pl.BlockSpec(shape, lambda *_: (0,) * nd, pipeline_mode=pl.Buffered(1))


def _dot(a, b):
    return jnp.dot(a, b, preferred_element_type=F32)


def _dot_nt(a, b):
    return lax.dot_general(a, b, (((1,), (1,)), ((), ())), preferred_element_type=F32)


def _split2(x):
    hi = x.astype(BF16)
    lo = (x - hi.astype(F32)).astype(BF16)
    return hi, lo


def _split3(x):
    hi = x.astype(BF16)
    r1 = x - hi.astype(F32)
    mid = r1.astype(BF16)
    lo = (r1 - mid.astype(F32)).astype(BF16)
    return hi, mid, lo


def _rms_mod(x, g, shift, scale):
    y = x * lax.rsqrt(jnp.mean(x * x, axis=-1, keepdims=True) + NORM_EPS) * g
    return y * (1.0 + scale) + shift


def _head_sum_mat():
    r = lax.broadcasted_iota(jnp.int32, (LANES, LANES), 0) // HEAD_DIM
    c = lax.broadcasted_iota(jnp.int32, (LANES, LANES), 1) // HEAD_DIM
    return (r == c).astype(BF16)


def _head_sum(x, ones_bd2):
    hi, lo = _split2(x)
    return _dot(jnp.concatenate([hi, lo], axis=1), ones_bd2)


def _ada_kernel(cc_ref, w_ref, b_ref, o_ref):
    cc = cc_ref[...]
    s = cc * jax.nn.sigmoid(cc)
    o_ref[0] = jnp.dot(s, w_ref[0], preferred_element_type=F32,
                       precision=lax.Precision.HIGHEST) + b_ref[0]


def _ada_table(cc, ada_w, ada_b):
    depth, d, n = ada_w.shape
    rows = cc.shape[0]
    tn = 1536
    return pl.pallas_call(
        _ada_kernel,
        grid=(depth, n // tn),
        in_specs=[pl.BlockSpec((rows, d), lambda i, j: (0, 0)),
                  pl.BlockSpec((1, d, tn), lambda i, j: (i, 0, j)),
                  pl.BlockSpec((1, 1, tn), lambda i, j: (i, 0, j))],
        out_specs=pl.BlockSpec((1, rows, tn), lambda i, j: (i, 0, j)),
        out_shape=jax.ShapeDtypeStruct((depth, rows, n), F32),
        compiler_params=_cparams(("arbitrary", "arbitrary")),
        name="ada_table",
    )(cc, ada_w, ada_b.reshape(depth, 1, n))


class _Geom:
    def __init__(self, B, L, C):
        self.B, self.L, self.C = B, L, C
        self.n_lat = B * L
        self.n_ctx = B * C
        self.rows = self.n_lat + self.n_ctx
        tm = 1024
        while (L % tm) or (self.n_ctx % tm):
            tm //= 2
        self.tm = tm

    def seg(self, i, tm):
        r0 = i * tm
        return jnp.where(r0 < self.n_lat, r0 // self.L, self.B)


def _mm_kernel(*refs, n_pro, n_epi, prologue, epilogue):
    x_ref = refs[0]
    pro_refs = refs[1:1 + n_pro]
    w_ref = refs[1 + n_pro]
    epi_refs = refs[2 + n_pro:2 + n_pro + n_epi]
    o_ref = refs[2 + n_pro + n_epi]
    j = pl.program_id(1)
    if prologue is not None:
        h_ref = refs[3 + n_pro + n_epi]

        @pl.when(j == 0)
        def _():
            h_ref[...] = prologue(x_ref, pro_refs).astype(BF16)

        lhs = h_ref[...]
    else:
        lhs = x_ref[...]
    acc = _dot(lhs, w_ref[...])
    epilogue(o_ref, acc, epi_refs, j)


def _mm_call(name, x, w, *, n_rows, tm, tn, prologue, pro_inputs, epilogue, epi_inputs,
             out_dtype):
    K, N = w.shape
    in_specs = [pl.BlockSpec((tm, K), lambda i, j: (i, 0))]
    args = [x]
    for a, spec in pro_inputs:
        in_specs.append(spec)
        args.append(a)
    in_specs.append(pl.BlockSpec((K, tn), lambda i, j: (0, j)))
    args.append(w)
    for a, spec in epi_inputs:
        in_specs.append(spec)
        args.append(a)
    scratch = [pltpu.VMEM((tm, K), BF16)] if prologue is not None else []
    kern = functools.partial(_mm_kernel, n_pro=len(pro_inputs), n_epi=len(epi_inputs),
                             prologue=prologue, epilogue=epilogue)
    return pl.pallas_call(
        kern, grid=(n_rows // tm, N // tn), in_specs=in_specs,
        out_specs=pl.BlockSpec((tm, tn), lambda i, j: (i, j)),
        out_shape=jax.ShapeDtypeStruct((n_rows, N), out_dtype),
        scratch_shapes=scratch,
        compiler_params=_cparams(("arbitrary", "arbitrary")),
        name=name,
    )(*args)


def _mod_full_spec(geom, tm):
    return pl.BlockSpec((1, MOD_ROWS, D_MODEL), lambda i, j: (geom.seg(i, tm), 0, 0))


def _mod_tile_spec(geom, tm, tn):
    return pl.BlockSpec((1, MOD_ROWS, tn), lambda i, j: (geom.seg(i, tm), 0, j))


def _normmod_prologue(shift_row, scale_row):
    def prologue(x_ref, pro_refs):
        mod_ref, g_ref = pro_refs
        m = mod_ref[0]
        return _rms_mod(x_ref[...], g_ref[...], m[shift_row:shift_row + 1],
                        m[scale_row:scale_row + 1])
    return prologue


def _rope_epilogue(n_rope_tiles, tn):
    def epilogue(o_ref, acc, epi_refs, j):
        cos_ref, sin_ref = epi_refs

        @pl.when(j < n_rope_tiles)
        def _():
            f = jnp.where(j < n_rope_tiles // 2, HEAD_DIM ** -0.5 * math.log2(math.e), 1.0)
            cos = cos_ref[...] * f
            sin = sin_ref[...] * f
            lane = lax.broadcasted_iota(jnp.int32, cos.shape, 1)
            low = (lane & 16) == 0
            for s in range(tn // LANES):
                xs = acc[:, s * LANES:(s + 1) * LANES]
                partner = jnp.where(low, pltpu.roll(xs, LANES - 16, 1), pltpu.roll(xs, 16, 1))
                o_ref[:, s * LANES:(s + 1) * LANES] = (xs * cos + partner * sin).astype(o_ref.dtype)

        @pl.when(j >= n_rope_tiles)
        def _():
            o_ref[...] = acc.astype(o_ref.dtype)
    return epilogue


def _gated_residual_epilogue(gate_row):
    def epilogue(o_ref, acc, epi_refs, j):
        x_ref, mod_ref = epi_refs
        gate = mod_ref[0][gate_row:gate_row + 1]
        o_ref[...] = x_ref[...] + gate * acc
    return epilogue


def _gated_residual_norm_epilogue(gate_row):
    def epilogue(o_ref, acc, epi_refs, j):
        x_ref, mod_ref, g_ref = epi_refs
        gate = mod_ref[0][gate_row:gate_row + 1]
        x = x_ref[...] + gate * acc
        o_ref[...] = x * lax.rsqrt(jnp.mean(x * x, axis=-1, keepdims=True) + NORM_EPS) * g_ref[...]
    return epilogue


def _relu2_epilogue(o_ref, acc, epi_refs, j):
    a = jnp.maximum(acc, 0.0)
    o_ref[...] = (a * a).astype(o_ref.dtype)


def _rwkv_feat_kernel(*refs, tm, seg_lat, seg_ctx, n_lat, has_vres):
    (x_ref, xp_ref, xn_ref, mod_ref, ng_ref, mix_ref, wrkv_ref, w1_ref, w2f_ref, w2b_ref,
     g1_ref, g2f_ref, g2b_ref, a1_ref, a2_ref, vec_ref) = refs[:16]
    pos = 16
    if has_vres:
        v1_ref, v2_ref, vfirst_ref = refs[pos:pos + 3]
        pos += 3
    (r_o, k_o, v_o, kk_o, a_o, lw_o, g_o) = refs[pos:pos + 7]

    i = pl.program_id(0)
    m = mod_ref[0]
    shift, scale = m[0:1], m[1:2]
    g = ng_ref[...]
    h = _rms_mod(x_ref[...], g, shift, scale)
    hp = _rms_mod(xp_ref[...], g, shift, scale)[7:8]
    hn = _rms_mod(xn_ref[...], g, shift, scale)[0:1]

    row = lax.broadcasted_iota(jnp.int32, (tm, 1), 0)
    rid = row + i * tm
    in_lat = rid < n_lat
    seg_pos = jnp.where(in_lat, rid % seg_lat, (rid - n_lat) % seg_ctx)
    seg_len = jnp.where(in_lat, seg_lat, seg_ctx)
    prev = pltpu.roll(h, 1, 0)
    prev = jnp.where(row == 0, hp, prev)
    prev = jnp.where(seg_pos == 0, 0.0, prev)
    nxt = pltpu.roll(h, tm - 1, 0)
    nxt = jnp.where(row == tm - 1, hn, nxt)
    nxt = jnp.where(seg_pos == seg_len - 1, 0.0, nxt)
    xx = 0.5 * (prev + nxt) - h

    def mixed(mi):
        return (h + xx * mix_ref[mi:mi + 1]).astype(BF16)

    vec = vec_ref[...]
    ones_bd = _head_sum_mat()
    ones_bd2 = jnp.concatenate([ones_bd, ones_bd], axis=0)

    def store(o_ref, val):
        for p in range(N_SLAB):
            o_ref[p] = val[:, p * LANES:(p + 1) * LANES].astype(o_ref.dtype)

    xr = mixed(0)
    store(r_o, _dot(xr, wrkv_ref[0]))

    xw = mixed(1)
    tw = jnp.tanh(_dot(xw, w1_ref[...])).astype(BF16)
    neg_e = -math.exp(-0.5)
    store(lw_o.at[0], neg_e * jax.nn.sigmoid(vec[0:1] + _dot(tw, w2f_ref[...])))
    store(lw_o.at[1], neg_e * jax.nn.sigmoid(vec[1:2] + _dot(tw, w2b_ref[...])))

    xg = mixed(5)
    sg = jax.nn.sigmoid(_dot(xg, g1_ref[...])).astype(BF16)
    half = sg.shape[1] // 2
    store(g_o.at[0], _dot(sg[:, :half], g2f_ref[...]))
    store(g_o.at[1], _dot(sg[:, half:], g2b_ref[...]))

    xa = mixed(4)
    a = jax.nn.sigmoid(vec[2:3] + _dot(_dot(xa, a1_ref[...]).astype(BF16), a2_ref[...]))
    store(a_o, a)

    xv = mixed(3)
    v = _dot(xv, wrkv_ref[2])
    if has_vres:
        gate = jax.nn.sigmoid(vec[5:6] + _dot(_dot(xv, v1_ref[...]).astype(BF16), v2_ref[...]))
        for p in range(N_SLAB):
            sl = slice(p * LANES, (p + 1) * LANES)
            vp = v[:, sl]
            v_o[p] = (vp + (vfirst_ref[p].astype(F32) - vp) * gate[:, sl]).astype(v_o.dtype)
    else:
        store(v_o, v)

    xk = mixed(2)
    k = _dot(xk, wrkv_ref[1])
    k_scale = 1.0 + (a - 1.0) * vec[4:5]
    kraw = k * vec[3:4]
    for p in range(N_SLAB):
        sl = slice(p * LANES, (p + 1) * LANES)
        kr = kraw[:, sl]
        ss = _head_sum(kr * kr, ones_bd2)
        kk_o[p] = (kr * lax.rsqrt(jnp.maximum(ss, 1e-24))).astype(kk_o.dtype)
        k_o[p] = (k[:, sl] * k_scale[:, sl]).astype(k_o.dtype)


def _rwkv_features(geom, x, mod, wts, vfirst):
    tm = 256
    R = geom.rows
    has_vres = vfirst is not None
    nblk8 = R // 8
    pm = functools.partial
    in_specs = [
        pl.BlockSpec((tm, D_MODEL), lambda i: (i, 0)),
        pl.BlockSpec((8, D_MODEL), lambda i: (jnp.maximum(i * (tm // 8) - 1, 0), 0)),
        pl.BlockSpec((8, D_MODEL), lambda i: (jnp.minimum((i + 1) * (tm // 8), nblk8 - 1), 0)),
        pl.BlockSpec((1, MOD_ROWS, D_MODEL), lambda i: (geom.seg(i, tm), 0, 0)),
    ]
    args = [x, x, x, mod]
    names = ["norm_g", "mix", "wrkv", "w1", "w2f", "w2b", "g1", "g2f", "g2b", "a1", "a2", "vec"]
    if has_vres:
        names += ["v1", "v2"]
    for nme in names:
        a = wts[nme]
        in_specs.append(_const_spec(a.shape))
        args.append(a)
    pair_spec = pl.BlockSpec((N_SLAB, tm, LANES), lambda i: (0, i, 0))
    dir_spec = pl.BlockSpec((2, N_SLAB, tm, LANES), lambda i: (0, 0, i, 0))
    if has_vres:
        in_specs.append(pair_spec)
        args.append(vfirst)
    pair_shape = jax.ShapeDtypeStruct((N_SLAB, R, LANES), BF16)
    lw_shape = jax.ShapeDtypeStruct((2, N_SLAB, R, LANES), F32)
    g_shape = jax.ShapeDtypeStruct((2, N_SLAB, R, LANES), BF16)
    kern = pm(_rwkv_feat_kernel, tm=tm, seg_lat=geom.L, seg_ctx=geom.C, n_lat=geom.n_lat,
              has_vres=has_vres)
    return pl.pallas_call(
        kern, grid=(R // tm,), in_specs=in_specs,
        out_specs=[pair_spec] * 5 + [dir_spec] * 2, out_shape=[pair_shape] * 5 + [lw_shape, g_shape],
        compiler_params=_cparams(("arbitrary",)),
        name="rwkv_features",
    )(*args)


def _bd_rows(x, lane_lo):
    return jnp.concatenate([jnp.where(lane_lo, x, 0.0), jnp.where(lane_lo, 0.0, x)], axis=0)


def _wkv_kernel(*refs, n_chunk):
    in_refs = (refs[0:6], refs[6:12])
    o_refs = refs[12:14]
    st_ref = refs[14]
    j = pl.program_id(1)
    c = CHUNK

    @pl.when(j == 0)
    def _():
        st_ref[...] = jnp.zeros_like(st_ref)

    t_i = lax.broadcasted_iota(jnp.int32, (c, LANES), 0)
    lane = lax.broadcasted_iota(jnp.int32, (c, LANES), 1)
    s_i = lane % c
    lane_lo = lane < c
    eye = t_i == s_i
    blk = [(t_i // n) == (s_i // n) for n in (8, 16, 32)]
    tc = lax.broadcasted_iota(jnp.int32, (c, 3 * c), 0)
    sc = lax.broadcasted_iota(jnp.int32, (c, 3 * c), 1) % c
    row1 = lax.broadcasted_iota(jnp.int32, (c, 1), 0)
    strict_d = (s_i < t_i, s_i > t_i)
    incl_d = (s_i <= t_i, s_i >= t_i)
    mcum_d = ((sc <= tc).astype(BF16), (sc >= tc).astype(BF16))
    is_last_d = (row1 == c - 1, row1 == 0)
    bd_avg = _head_sum_mat() * (1.0 / HEAD_DIM)
    bd_avg2 = jnp.concatenate([bd_avg, bd_avg], axis=0)
    kr = lax.broadcasted_iota(jnp.int32, (LANES, LANES), 0)
    kc = lax.broadcasted_iota(jnp.int32, (LANES, LANES), 1)
    same_head = (kr // HEAD_DIM) == (kc // HEAD_DIM)
    diag = kr == kc
    eye_f = jnp.where(eye, 1.0, 0.0)

    def bf(x):
        return x.astype(BF16)

    def pmul(x_pair, y_pair):
        return _dot(bf(x_pair), bf(_bd_rows(y_pair, lane_lo)))

    def chunk_body(ci, carry):
        ds_d = (pl.ds(pl.multiple_of(ci * c, c), c),
                pl.ds(pl.multiple_of((n_chunk - 1 - ci) * c, c), c))
        chains = [(d, p) for d in range(2) for p in range(N_SLAB)]
        strict = [strict_d[d] for d, _ in chains]
        incl = [incl_d[d] for d, _ in chains]

        def each(fn, *lists):
            return [fn(*args) for args in zip(*lists)]

        def load(which):
            return [in_refs[d][which][0, p, ds_d[d], :] if which == 5
                    else in_refs[d][which][p, ds_d[d], :].astype(F32) for d, p in chains]

        lw = load(5)
        cum = [_dot(mcum_d[d], jnp.concatenate(_split3(x), axis=0))
               for (d, _), x in zip(chains, lw)]
        e_pos = each(jnp.exp, cum)
        e_neg = each(lambda x: jnp.exp(-x), cum)
        e_exc = each(lambda x, y: jnp.exp(x - y), cum, lw)
        p_end = [jnp.sum(jnp.where(is_last_d[d], e, 0.0), axis=0, keepdims=True)
                 for (d, _), e in zip(chains, e_pos)]
        r_t = each(lambda x, e: x * e, load(0), e_pos)
        kk = load(3)
        a_t = each(lambda x, e: -x * e, kk, e_exc)
        b_t = each(lambda x, y, e: x * y * e, kk, load(4), e_neg)
        k_t = each(lambda x, e: x * e, load(1), e_neg)
        v = load(2)
        gram = each(lambda at, rt, bt, kt: _dot_nt(
            bf(jnp.concatenate([at, rt], axis=0)),
            bf(jnp.concatenate([_bd_rows(bt, lane_lo), _bd_rows(kt, lane_lo)], axis=0))),
            a_t, r_t, b_t, k_t)
        l_ab = each(lambda g, m: jnp.where(m, g[:c, :LANES], 0.0), gram, strict)
        l_ak = each(lambda g, m: jnp.where(m, g[:c, LANES:], 0.0), gram, strict)
        p_rb = each(lambda g, m: jnp.where(m, g[c:, :LANES], 0.0), gram, incl)
        p_rk = each(lambda g, m: jnp.where(m, g[c:, LANES:], 0.0), gram, incl)

        n0 = each(lambda x: jnp.where(blk[0], x, 0.0), l_ab)
        p1 = each(pmul, n0, n0)
        t0 = each(lambda x: eye_f + x, n0)
        both = each(lambda t, q: pmul(jnp.concatenate([t, q], axis=0), q), t0, p1)
        t1 = each(lambda t, bo: t + bo[:c], t0, both)
        tinv = each(lambda t, bo: t + pmul(t, bo[c:]), t1, both)
        prev_blk = blk[0]
        for nb in (blk[1], blk[2], None):
            sel = jnp.logical_not(prev_blk) if nb is None else (nb & jnp.logical_not(prev_blk))
            mid = each(lambda x, t: pmul(jnp.where(sel, x, 0.0), t), l_ab, tinv)
            tinv = each(lambda t, m: t + pmul(t, m), tinv, mid)
            prev_blk = nb

        st = [st_ref[d, p] for d, p in chains]
        st_b = each(bf, st)
        v_bd = each(lambda x: bf(_bd_rows(x, lane_lo)), v)
        rhs_u = each(lambda at, lk, sb, vb: _dot(bf(jnp.concatenate([at, lk], axis=1)),
                                                 jnp.concatenate([sb, vb], axis=0)),
                     a_t, l_ak, st_b, v_bd)
        u_p = each(pmul, tinv, rhs_u)
        o = each(lambda rt, prb, prk, sb, up, vb: _dot(
            bf(jnp.concatenate([rt, prb, prk], axis=1)),
            jnp.concatenate([sb, bf(_bd_rows(up, lane_lo)), vb], axis=0)),
            r_t, p_rb, p_rk, st_b, u_p, v_bd)

        upd = each(lambda bt, kt, pe, up, vv: _dot(
            bf(jnp.concatenate([bt * pe, kt * pe], axis=0).T),
            bf(jnp.concatenate([up, vv], axis=0))),
            b_t, k_t, p_end, u_p, v)
        for i, (d, p) in enumerate(chains):
            p_col = jnp.sum(jnp.where(diag, p_end[i], 0.0), axis=1, keepdims=True)
            st_ref[d, p] = st[i] * p_col + jnp.where(same_head, upd[i], 0.0)

        mu = each(lambda x: _dot(jnp.concatenate(_split2(x), axis=1), bf(bd_avg2)), o)
        dlt = each(lambda x, m: x - m, o, mu)
        var = each(lambda x: _dot(jnp.concatenate(_split2(x * x), axis=1), bf(bd_avg2)), dlt)
        for i, (d, p) in enumerate(chains):
            o_refs[d][p, ds_d[d], :] = (dlt[i] * lax.rsqrt(var[i] + GN_EPS)).astype(o_refs[d].dtype)
        return carry

    lax.fori_loop(0, n_chunk, chunk_body, 0)


def _wkv_scan(geom, r, k, v, kk, a, lw):
    B, L, C = geom.B, geom.L, geom.C
    blk = SCAN_BLOCK
    n_lat_blk = L // blk
    n_ctx_blk = C // blk
    n_steps = n_lat_blk + n_ctx_blk
    lat_blocks = geom.n_lat // blk

    def row_block(d, b, j):
        in_ctx = j < n_ctx_blk
        jc = jnp.where(d == 0, j, n_ctx_blk - 1 - j)
        jl = jnp.where(d == 0, j - n_ctx_blk, n_steps - 1 - j)
        return jnp.where(in_ctx, lat_blocks + b * n_ctx_blk + jc, b * n_lat_blk + jl)

    def specs(d):
        pair = pl.BlockSpec((N_SLAB, blk, LANES), lambda b, j: (0, row_block(d, b, j), 0))
        lw_spec = pl.BlockSpec((1, N_SLAB, blk, LANES), lambda b, j: (d, 0, row_block(d, b, j), 0))
        return pair, lw_spec

    (pair_f, lw_f), (pair_b, lw_b) = specs(0), specs(1)
    kern = functools.partial(_wkv_kernel, n_chunk=blk // CHUNK)
    out_shape = jax.ShapeDtypeStruct((N_SLAB, geom.rows, LANES), BF16)
    return pl.pallas_call(
        kern, grid=(B, n_steps),
        in_specs=[pair_f] * 5 + [lw_f] + [pair_b] * 5 + [lw_b],
        out_specs=[pair_f, pair_b],
        out_shape=[out_shape, out_shape],
        scratch_shapes=[pltpu.VMEM((2, N_SLAB, LANES, LANES), F32)],
        compiler_params=_cparams(("arbitrary", "arbitrary")),
        name="wkv_scan",
    )(r, k, v, kk, a, lw, r, k, v, kk, a, lw)


def _rwkv_out_kernel(onf_ref, onb_ref, r_ref, k_ref, v_ref, g_ref, vec_ref, w_ref, x_ref, mod_ref,
                     o_ref, h_ref):
    j = pl.program_id(1)

    @pl.when(j == 0)
    def _():
        ones_bd = _head_sum_mat()
        ones_bd2 = jnp.concatenate([ones_bd, ones_bd], axis=0)
        vec = vec_ref[...]
        for p in range(N_SLAB):
            sl = slice(p * LANES, (p + 1) * LANES)
            rk, lg, lb = vec[0:1, sl], vec[1:2, sl], vec[2:3, sl]
            f32 = lambda ref, *idx: ref[idx].astype(F32)
            bonus = _head_sum(f32(r_ref, p) * f32(k_ref, p) * rk, ones_bd2) * f32(v_ref, p)
            y = ((f32(onf_ref, p) * lg + lb + bonus) * f32(g_ref, 0, p)
                 + (f32(onb_ref, p) * lg + lb + bonus) * f32(g_ref, 1, p))
            h_ref[:, sl] = y.astype(BF16)

    acc = _dot(h_ref[...], w_ref[...])
    o_ref[...] = x_ref[...] + mod_ref[0][2:3] * acc


def _rwkv_out(geom, on_f, on_b, r, k, v, g, vec, w_o, x, mod):
    tm, tn = 512, 512
    R = geom.rows
    pair_spec = pl.BlockSpec((N_SLAB, tm, LANES), lambda i, j: (0, i, 0))
    dir_spec = pl.BlockSpec((2, N_SLAB, tm, LANES), lambda i, j: (0, 0, i, 0))
    return pl.pallas_call(
        _rwkv_out_kernel, grid=(R // tm, D_MODEL // tn),
        in_specs=[pair_spec, pair_spec, pair_spec, pair_spec, pair_spec, dir_spec,
                  _const_spec(vec.shape),
                  pl.BlockSpec((D_MODEL, tn), lambda i, j: (0, j)),
                  pl.BlockSpec((tm, tn), lambda i, j: (i, j)),
                  pl.BlockSpec((1, MOD_ROWS, tn), lambda i, j: (geom.seg(i, tm), 0, j))],
        out_specs=pl.BlockSpec((tm, tn), lambda i, j: (i, j)),
        out_shape=jax.ShapeDtypeStruct((R, D_MODEL), F32),
        scratch_shapes=[pltpu.VMEM((tm, D_MODEL), BF16)],
        compiler_params=_cparams(("arbitrary", "arbitrary")),
        name="rwkv_out",
    )(on_f, on_b, r, k, v, g, vec, w_o, x, mod)


def _attn_kernel(*refs, tq, tk, n_lat_chunks, lambda_init):
    if n_lat_chunks:
        (q_ref, kc_ref, vc_ref, kl_ref, vl_ref, lam_ref, sg_ref, o_ref,
         m_ref, l_ref, acc_ref, al_ref, sc_ref, p_ref, vtc_ref, vtl_ref) = refs
    else:
        (q_ref, kc_ref, vc_ref, lam_ref, sg_ref, o_ref,
         m_ref, l_ref, acc_ref, al_ref, sc_ref, p_ref, vtc_ref) = refs
    n_ctx = kc_ref.shape[0]

    @pl.when(pl.program_id(2) == 0)
    def _():
        vtc_ref[...] = vc_ref[...].astype(F32).T.astype(BF16)
        for i in range(n_lat_chunks):
            vtl_ref[i] = vl_ref[i * tk:(i + 1) * tk, :].astype(F32).T.astype(BF16)

    lane = lax.broadcasted_iota(jnp.int32, (tq, LANES), 1)
    q = q_ref[...]
    zero = jnp.zeros_like(q)
    qs = (jnp.where(lane < HEAD_DIM, q, zero), jnp.where(lane < HEAD_DIM, zero, q))

    m_ref[...] = jnp.full_like(m_ref, -jnp.inf)
    l_ref[...] = jnp.zeros_like(l_ref)
    acc_ref[...] = jnp.zeros_like(acc_ref)

    n_qt = tq // LANES

    def scores(slot, k):
        nk = k.shape[0]
        for s in range(2):
            res = _dot_nt(k, qs[s])
            for t in range(n_qt):
                sc_ref[slot, s, t, 0:nk, :] = res[:, t * LANES:(t + 1) * LANES]

    def consume(slot, nk, vt):
        for s in range(2):
            for t in range(n_qt):
                cols = slice(t * LANES, (t + 1) * LANES)
                x = sc_ref[slot, s, t, 0:nk, :]
                m_old = m_ref[s, :, cols]
                m_new = jnp.maximum(m_old, jnp.max(x, axis=0, keepdims=True))
                alpha = jnp.exp2(m_old - m_new)
                pr = jnp.exp2(x - m_new)
                l_ref[s, :, cols] = alpha * l_ref[s, :, cols] + jnp.sum(pr, axis=0, keepdims=True)
                m_ref[s, :, cols] = m_new
                al_ref[s, :, cols] = alpha
                p_ref[s, t, 0:nk, :] = pr.astype(BF16)
            pt = jnp.concatenate([p_ref[s, t, 0:nk, :] for t in range(n_qt)], axis=1)
            acc_ref[s] = al_ref[s] * acc_ref[s] + _dot(vt, pt)

    scores(0, kc_ref[...])
    consume(0, n_ctx, vtc_ref[...])

    if n_lat_chunks:
        def kblk(i):
            return kl_ref[pl.ds(pl.multiple_of(i * tk, tk), tk), :]

        def vblk(i):
            return vtl_ref[i]

        scores(0, kblk(0))
        n_pairs = (n_lat_chunks - 1) // 2

        def body(i, carry):
            scores(1, kblk(2 * i + 1))
            consume(0, tk, vblk(2 * i))
            scores(0, kblk(2 * i + 2))
            consume(1, tk, vblk(2 * i + 1))
            return carry

        lax.fori_loop(0, n_pairs, body, 0)
        done = 2 * n_pairs
        if n_lat_chunks - done == 2:
            scores(1, kblk(done + 1))
            consume(0, tk, vblk(done))
            consume(1, tk, vblk(done + 1))
        else:
            consume(0, tk, vblk(done))

    lv = lam_ref[...]
    lam = (jnp.exp(jnp.sum(lv[0:1] * lv[1:2], axis=-1, keepdims=True))
           - jnp.exp(jnp.sum(lv[2:3] * lv[3:4], axis=-1, keepdims=True)) + lambda_init)
    ot = acc_ref[0] / l_ref[0] - lam * (acc_ref[1] / l_ref[1])
    ot = ot * lax.rsqrt(jnp.mean(ot * ot, axis=0, keepdims=True) + SUBLN_EPS)
    o_ref[...] = (ot.T * sg_ref[0:1] * (1.0 - lambda_init)).astype(o_ref.dtype)


def _attention(geom, qkv, lam_vec, subln_g, lambda_init, *, need_ctx):
    B, L, C = geom.B, geom.L, geom.C
    n_heads = D_MODEL // LANES
    ctx_row_blk = geom.n_lat // C
    kc_spec = pl.BlockSpec((C, LANES), lambda b, h, qi: (ctx_row_blk + b, n_heads + h))
    vc_spec = pl.BlockSpec((C, LANES), lambda b, h, qi: (ctx_row_blk + b, 2 * n_heads + h))
    small = [_const_spec(lam_vec.shape), _const_spec(subln_g.shape)]

    def call(name, tq, q_map, n_q, tk, n_lat_chunks, extra_specs, rows_out):
        kern = functools.partial(_attn_kernel, tq=tq, tk=tk, n_lat_chunks=n_lat_chunks,
                                 lambda_init=lambda_init)
        n_kv = 2 + len(extra_specs)
        return pl.pallas_call(
            kern, grid=(B, n_heads, n_q),
            in_specs=[pl.BlockSpec((tq, LANES), q_map), kc_spec, vc_spec] + extra_specs + small,
            out_specs=pl.BlockSpec((tq, LANES), lambda b, h, qi: (q_map(b, h, qi)[0] - rows_out[0], h)),
            out_shape=jax.ShapeDtypeStruct((rows_out[1], D_MODEL), BF16),
            scratch_shapes=[pltpu.VMEM((2, 1, tq), F32),
                            pltpu.VMEM((2, 1, tq), F32),
                            pltpu.VMEM((2, LANES, tq), F32),
                            pltpu.VMEM((2, 1, tq), F32),
                            pltpu.VMEM((2, 2, tq // LANES, max(tk, C), LANES), F32),
                            pltpu.VMEM((2, tq // LANES, max(tk, C), LANES), BF16),
                            pltpu.VMEM((LANES, C), BF16)]
            + ([pltpu.VMEM((n_lat_chunks, LANES, tk), BF16)] if n_lat_chunks else []),
            compiler_params=_cparams(("arbitrary", "arbitrary", "arbitrary")),
            name=name,
        )(*([qkv] * (1 + n_kv) + [lam_vec, subln_g]))

    tq = min(ATTN_TQ, L)
    tk = min(ATTN_TK, L)
    lat_specs = [pl.BlockSpec((L, LANES), lambda b, h, qi: (b, n_heads + h)),
                 pl.BlockSpec((L, LANES), lambda b, h, qi: (b, 2 * n_heads + h))]
    o_lat = call("diff_attention", tq, lambda b, h, qi: (b * (L // tq) + qi, h), L // tq, tk, L // tk,
                 lat_specs, (0, geom.n_lat))
    if not need_ctx:
        return o_lat
    o_ctx = call("diff_attention_ctx", C, lambda b, h, qi: (ctx_row_blk + b, h), 1, tk, 0, [],
                 (ctx_row_blk, geom.n_ctx))
    return jnp.concatenate([o_lat, o_ctx], axis=0)


def _rope_tables(geom):
    L = geom.L
    t = jnp.arange(L, dtype=jnp.int32)
    inv = ROPE_THETA ** (-jnp.arange(ROPE_PAIRS, dtype=F32) / ROPE_PAIRS)
    lane = jnp.arange(LANES, dtype=jnp.int32) % HEAD_DIM
    freq = inv[lane % ROPE_PAIRS]
    pos = jnp.where((lane < HEAD_DIM // 2)[None, :], (t // GRID_W)[:, None], (t % GRID_W)[:, None])
    ang = pos.astype(F32) * freq[None, :]
    sign = jnp.where((lane % 32) < 16, -1.0, 1.0)
    cos_l = jnp.cos(ang)
    sin_l = jnp.sin(ang) * sign[None, :]
    cos = jnp.concatenate([cos_l, jnp.ones((geom.tm, LANES), F32)], axis=0)
    sin = jnp.concatenate([sin_l, jnp.zeros((geom.tm, LANES), F32)], axis=0)
    return cos, sin


def _pad_cols(w, n):
    return jnp.pad(w, ((0, 0), (0, n - w.shape[1])))


def _pad_rows(w, n, offset=0):
    return jnp.pad(w, ((offset, n - offset - w.shape[0]), (0, 0)))


def _rwkv_weights(j, norm_g_row, rw_mix, rw_w_rkv, rw_w0, rw_w1, rw_w2, rw_a0, rw_a1, rw_a2, rw_g1,
                  rw_g2, rw_kk, rw_ka, rw_v0, rw_v1, rw_v2):
    lora_w = rw_w1.shape[-1]
    g_pad = 2 * LANES
    wts = {
        "norm_g": norm_g_row.reshape(1, D_MODEL),
        "mix": jnp.pad(rw_mix[j], ((0, 2), (0, 0))),
        "wrkv": rw_w_rkv[j].astype(BF16),
        "w1": jnp.concatenate([rw_w1[j, 0], rw_w1[j, 1]], axis=1).astype(BF16),
        "w2f": _pad_rows(rw_w2[j, 0], LANES).astype(BF16),
        "w2b": _pad_rows(rw_w2[j, 1], LANES, lora_w).astype(BF16),
        "g1": jnp.concatenate([_pad_cols(rw_g1[j, 0], g_pad), _pad_cols(rw_g1[j, 1], g_pad)],
                              axis=1).astype(BF16),
        "g2f": _pad_rows(rw_g2[j, 0], g_pad).astype(BF16),
        "g2b": _pad_rows(rw_g2[j, 1], g_pad).astype(BF16),
        "a1": _pad_cols(rw_a1[j], LANES).astype(BF16),
        "a2": _pad_rows(rw_a2[j], LANES).astype(BF16),
    }
    v0 = rw_v0[j - 1] if j > 0 else jnp.zeros((D_MODEL,), F32)
    wts["vec"] = jnp.stack([rw_w0[j, 0], rw_w0[j, 1], rw_a0[j], rw_kk[j], rw_ka[j], v0,
                            jnp.zeros((D_MODEL,), F32), jnp.zeros((D_MODEL,), F32)], axis=0)
    if j > 0:
        wts["v1"] = _pad_cols(rw_v1[j - 1], LANES).astype(BF16)
        wts["v2"] = _pad_rows(rw_v2[j - 1], LANES).astype(BF16)
    return wts


def kernel(x, c, ctx, c_ctx, ada_w, ada_b, norm_g, final_g, rw_mix, rw_w_rkv, rw_w0, rw_w1, rw_w2, rw_a0, rw_a1, rw_a2, rw_g1, rw_g2, rw_kk, rw_ka, rw_rk, rw_ln_g, rw_ln_b, rw_w_o, rw_v0, rw_v1, rw_v2, da_w_qkv, da_w_o, da_lq1, da_lk1, da_lq2, da_lk2, da_subln_g, mlp_w1, mlp_w2):
    B, L, D = x.shape
    C = ctx.shape[1]
    depth = ada_w.shape[0]
    geom = _Geom(B, L, C)
    tm = geom.tm

    n_seg = 16
    cc = jnp.concatenate([c, c_ctx[None, :], jnp.zeros((n_seg - B - 1, D), F32)], axis=0)
    mod_all = _ada_table(cc, ada_w, ada_b).reshape(depth, n_seg, 6, D)
    mod_all = jnp.pad(mod_all, ((0, 0), (0, 0), (0, MOD_ROWS - 6), (0, 0)))

    xs = jnp.concatenate([x.reshape(B * L, D), ctx.reshape(B * C, D)], axis=0)
    rope_cos, rope_sin = _rope_tables(geom)
    vfirst = None

    for i in range(depth):
        last = i == depth - 1
        mod = mod_all[i]
        j = i // 2
        ng0 = norm_g[i, 0].reshape(1, D)
        ng1 = norm_g[i, 1].reshape(1, D)
        n_rows = geom.n_lat if last else geom.rows
        tn = 512
        xtile = pl.BlockSpec((tm, tn), lambda ti, tj: (ti, tj))
        if i % 2 == 0:
            wts = _rwkv_weights(j, norm_g[i, 0], rw_mix, rw_w_rkv, rw_w0, rw_w1, rw_w2, rw_a0, rw_a1,
                                rw_a2, rw_g1, rw_g2, rw_kk, rw_ka, rw_v0, rw_v1, rw_v2)
            r, k, v, kk, a, lw, g = _rwkv_features(geom, xs, mod, wts, vfirst)
            if j == 0:
                vfirst = v
            on_f, on_b = _wkv_scan(geom, r, k, v, kk, a, lw)
            vec = jnp.stack([rw_rk[j].reshape(D), rw_ln_g[j], rw_ln_b[j]]
                            + [jnp.zeros((D,), F32)] * 5, axis=0)
            xs = _rwkv_out(geom, on_f, on_b, r, k, v, g, vec, rw_w_o[j].astype(BF16), xs, mod)
        else:
            lambda_init = 0.8 - 0.6 * math.exp(-0.3 * i)
            seq_tiles = L // tm
            rope_spec = pl.BlockSpec(
                (tm, LANES),
                lambda ti, tj: (jnp.where(ti < B * seq_tiles, ti % seq_tiles, seq_tiles), 0))
            qkv = _mm_call("attn_qkv", xs, da_w_qkv[j].astype(BF16), n_rows=geom.rows, tm=tm, tn=2 * tn,
                           prologue=_normmod_prologue(0, 1),
                           pro_inputs=[(mod, _mod_full_spec(geom, tm)), (ng0, _const_spec(ng0.shape))],
                           epilogue=_rope_epilogue(D // tn, 2 * tn),
                           epi_inputs=[(rope_cos, rope_spec), (rope_sin, rope_spec)],
                           out_dtype=BF16)
            lam_vec = jnp.pad(jnp.stack([da_lq1[j], da_lk1[j], da_lq2[j], da_lk2[j]], axis=0),
                              ((0, 4), (0, LANES - HEAD_DIM)))
            sg = da_subln_g[j].reshape(1, LANES)
            o_att = _attention(geom, qkv, lam_vec, sg, lambda_init, need_ctx=not last)
            xs = _mm_call("attn_out", o_att, da_w_o[j].astype(BF16), n_rows=n_rows, tm=tm, tn=D,
                          prologue=None, pro_inputs=[], epilogue=_gated_residual_epilogue(2),
                          epi_inputs=[(xs, pl.BlockSpec((tm, D), lambda ti, tj: (ti, 0))),
                                      (mod, _mod_tile_spec(geom, tm, D))],
                          out_dtype=F32)
        hmid = _mm_call("mlp_up", xs, mlp_w1[i].astype(BF16), n_rows=n_rows, tm=tm, tn=2 * tn,
                        prologue=_normmod_prologue(3, 4),
                        pro_inputs=[(mod, _mod_full_spec(geom, tm)), (ng1, _const_spec(ng1.shape))],
                        epilogue=_relu2_epilogue, epi_inputs=[], out_dtype=BF16)
        if last:
            tmf = min(tm, 512)
            fg = final_g.reshape(1, D)
            xs = _mm_call("mlp_down_final", hmid, mlp_w2[i].astype(BF16), n_rows=n_rows, tm=tmf, tn=D,
                          prologue=None, pro_inputs=[], epilogue=_gated_residual_norm_epilogue(5),
                          epi_inputs=[(xs, pl.BlockSpec((tmf, D), lambda ti, tj: (ti, 0))),
                                      (mod, _mod_tile_spec(geom, tmf, D)),
                                      (fg, _const_spec(fg.shape))],
                          out_dtype=F32)
        else:
            xs = _mm_call("mlp_down", hmid, mlp_w2[i].astype(BF16), n_rows=n_rows, tm=tm, tn=tn,
                          prologue=None, pro_inputs=[], epilogue=_gated_residual_epilogue(5),
                          epi_inputs=[(xs, xtile), (mod, _mod_tile_spec(geom, tm, tn))],
                          out_dtype=F32)
    return xs[:B * L].reshape(B, L, D)
```

```python
import functools
import math

import jax
import jax.numpy as jnp
from jax import lax
from jax.experimental import pallas as pl
from jax.experimental.pallas import tpu as pltpu

D_MODEL = 1024
HEAD_DIM = 64
LANES = 128
N_SLAB = D_MODEL // LANES
D_FF = 4 * D_MODEL
GRID_W = 64
ROPE_THETA = 10000.0
ROPE_PAIRS = HEAD_DIM // 4
NORM_EPS = 1e-6
SUBLN_EPS = 1e-5
GN_EPS = 64e-5
CHUNK = 64
SCAN_BLOCK = 256
ATTN_TQ = 2048
ATTN_TK = 512
MOD_ROWS = 8
VMEM_LIMIT = 56 * 1024 * 1024

F32 = jnp.float32
BF16 = jnp.bfloat16


def _cparams(sem):
    return pltpu.CompilerParams(dimension_semantics=sem, vmem_limit_bytes=VMEM_LIMIT)


def _const_spec(shape):
    nd = len(shape)
    return pl.BlockSpec(shape, lambda *_: (0,) * nd, pipeline_mode=pl.Buffered(1))


def _dot(a, b):
    return jnp.dot(a, b, preferred_element_type=F32)


def _dot_nt(a, b):
    return lax.dot_general(a, b, (((1,), (1,)), ((), ())), preferred_element_type=F32)


def _split2(x):
    hi = x.astype(BF16)
    lo = (x - hi.astype(F32)).astype(BF16)
    return hi, lo


def _split3(x):
    hi = x.astype(BF16)
    r1 = x - hi.astype(F32)
    mid = r1.astype(BF16)
    lo = (r1 - mid.astype(F32)).astype(BF16)
    return hi, mid, lo


def _rms_mod(x, g, shift, scale):
    y = x * lax.rsqrt(jnp.mean(x * x, axis=-1, keepdims=True) + NORM_EPS) * g
    return y * (1.0 + scale) + shift


def _head_sum_mat():
    r = lax.broadcasted_iota(jnp.int32, (LANES, LANES), 0) // HEAD_DIM
    c = lax.broadcasted_iota(jnp.int32, (LANES, LANES), 1) // HEAD_DIM
    return (r == c).astype(BF16)


def _head_sum(x, ones_bd2):
    hi, lo = _split2(x)
    return _dot(jnp.concatenate([hi, lo], axis=1), ones_bd2)


def _ada_kernel(cc_ref, w_ref, b_ref, o_ref):
    cc = cc_ref[...]
    s = cc * jax.nn.sigmoid(cc)
    o_ref[0] = jnp.dot(s, w_ref[0], preferred_element_type=F32,
                       precision=lax.Precision.HIGHEST) + b_ref[0]


def _ada_table(cc, ada_w, ada_b):
    depth, d, n = ada_w.shape
    rows = cc.shape[0]
    tn = 1536
    return pl.pallas_call(
        _ada_kernel,
        grid=(depth, n // tn),
        in_specs=[pl.BlockSpec((rows, d), lambda i, j: (0, 0)),
                  pl.BlockSpec((1, d, tn), lambda i, j: (i, 0, j)),
                  pl.BlockSpec((1, 1, tn), lambda i, j: (i, 0, j))],
        out_specs=pl.BlockSpec((1, rows, tn), lambda i, j: (i, 0, j)),
        out_shape=jax.ShapeDtypeStruct((depth, rows, n), F32),
        compiler_params=_cparams(("arbitrary", "arbitrary")),
        name="ada_table",
    )(cc, ada_w, ada_b.reshape(depth, 1, n))


class _Geom:
    def __init__(self, B, L, C):
        self.B, self.L, self.C = B, L, C
        self.n_lat = B * L
        self.n_ctx = B * C
        self.rows = self.n_lat + self.n_ctx
        tm = 1024
        while (L % tm) or (self.n_ctx % tm):
            tm //= 2
        self.tm = tm

    def seg(self, i, tm):
        r0 = i * tm
        return jnp.where(r0 < self.n_lat, r0 // self.L, self.B)


def _mm_kernel(*refs, n_pro, n_epi, prologue, epilogue):
    x_ref = refs[0]
    pro_refs = refs[1:1 + n_pro]
    w_ref = refs[1 + n_pro]
    epi_refs = refs[2 + n_pro:2 + n_pro + n_epi]
    o_ref = refs[2 + n_pro + n_epi]
    j = pl.program_id(1)
    if prologue is not None:
        h_ref = refs[3 + n_pro + n_epi]

        @pl.when(j == 0)
        def _():
            h_ref[...] = prologue(x_ref, pro_refs).astype(BF16)

        lhs = h_ref[...]
    else:
        lhs = x_ref[...]
    acc = _dot(lhs, w_ref[...])
    epilogue(o_ref, acc, epi_refs, j)


def _mm_call(name, x, w, *, n_rows, tm, tn, prologue, pro_inputs, epilogue, epi_inputs,
             out_dtype):
    K, N = w.shape
    in_specs = [pl.BlockSpec((tm, K), lambda i, j: (i, 0))]
    args = [x]
    for a, spec in pro_inputs:
        in_specs.append(spec)
        args.append(a)
    in_specs.append(pl.BlockSpec((K, tn), lambda i, j: (0, j)))
    args.append(w)
    for a, spec in epi_inputs:
        in_specs.append(spec)
        args.append(a)
    scratch = [pltpu.VMEM((tm, K), BF16)] if prologue is not None else []
    kern = functools.partial(_mm_kernel, n_pro=len(pro_inputs), n_epi=len(epi_inputs),
                             prologue=prologue, epilogue=epilogue)
    return pl.pallas_call(
        kern, grid=(n_rows // tm, N // tn), in_specs=in_specs,
        out_specs=pl.BlockSpec((tm, tn), lambda i, j: (i, j)),
        out_shape=jax.ShapeDtypeStruct((n_rows, N), out_dtype),
        scratch_shapes=scratch,
        compiler_params=_cparams(("arbitrary", "arbitrary")),
        name=name,
    )(*args)


def _mod_full_spec(geom, tm):
    return pl.BlockSpec((1, MOD_ROWS, D_MODEL), lambda i, j: (geom.seg(i, tm), 0, 0))


def _mod_tile_spec(geom, tm, tn):
    return pl.BlockSpec((1, MOD_ROWS, tn), lambda i, j: (geom.seg(i, tm), 0, j))


def _normmod_prologue(shift_row, scale_row):
    def prologue(x_ref, pro_refs):
        mod_ref, g_ref = pro_refs
        m = mod_ref[0]
        return _rms_mod(x_ref[...], g_ref[...], m[shift_row:shift_row + 1],
                        m[scale_row:scale_row + 1])
    return prologue


def _rope_epilogue(n_rope_tiles, tn):
    def epilogue(o_ref, acc, epi_refs, j):
        cos_ref, sin_ref = epi_refs

        @pl.when(j < n_rope_tiles)
        def _():
            f = jnp.where(j < n_rope_tiles // 2, HEAD_DIM ** -0.5 * math.log2(math.e), 1.0)
            cos = cos_ref[...] * f
            sin = sin_ref[...] * f
            lane = lax.broadcasted_iota(jnp.int32, cos.shape, 1)
            low = (lane & 16) == 0
            for s in range(tn // LANES):
                xs = acc[:, s * LANES:(s + 1) * LANES]
                partner = jnp.where(low, pltpu.roll(xs, LANES - 16, 1), pltpu.roll(xs, 16, 1))
                o_ref[:, s * LANES:(s + 1) * LANES] = (xs * cos + partner * sin).astype(o_ref.dtype)

        @pl.when(j >= n_rope_tiles)
        def _():
            o_ref[...] = acc.astype(o_ref.dtype)
    return epilogue


def _gated_residual_epilogue(gate_row):
    def epilogue(o_ref, acc, epi_refs, j):
        x_ref, mod_ref = epi_refs
        gate = mod_ref[0][gate_row:gate_row + 1]
        o_ref[...] = x_ref[...] + gate * acc
    return epilogue


def _gated_residual_norm_epilogue(gate_row):
    def epilogue(o_ref, acc, epi_refs, j):
        x_ref, mod_ref, g_ref = epi_refs
        gate = mod_ref[0][gate_row:gate_row + 1]
        x = x_ref[...] + gate * acc
        o_ref[...] = x * lax.rsqrt(jnp.mean(x * x, axis=-1, keepdims=True) + NORM_EPS) * g_ref[...]
    return epilogue


def _relu2_epilogue(o_ref, acc, epi_refs, j):
    a = jnp.maximum(acc, 0.0)
    o_ref[...] = (a * a).astype(o_ref.dtype)


def _rwkv_feat_kernel(*refs, tm, seg_lat, seg_ctx, n_lat, has_vres):
    (x_ref, xp_ref, xn_ref, mod_ref, ng_ref, mix_ref, wrkv_ref, w1_ref, w2f_ref, w2b_ref,
     g1_ref, g2f_ref, g2b_ref, a1_ref, a2_ref, vec_ref) = refs[:16]
    pos = 16
    if has_vres:
        v1_ref, v2_ref, vfirst_ref = refs[pos:pos + 3]
        pos += 3
    (r_o, k_o, v_o, kk_o, a_o, lw_o, g_o) = refs[pos:pos + 7]

    i = pl.program_id(0)
    m = mod_ref[0]
    shift, scale = m[0:1], m[1:2]
    g = ng_ref[...]
    h = _rms_mod(x_ref[...], g, shift, scale)
    hp = _rms_mod(xp_ref[...], g, shift, scale)[7:8]
    hn = _rms_mod(xn_ref[...], g, shift, scale)[0:1]

    row = lax.broadcasted_iota(jnp.int32, (tm, 1), 0)
    rid = row + i * tm
    in_lat = rid < n_lat
    seg_pos = jnp.where(in_lat, rid % seg_lat, (rid - n_lat) % seg_ctx)
    seg_len = jnp.where(in_lat, seg_lat, seg_ctx)
    prev = pltpu.roll(h, 1, 0)
    prev = jnp.where(row == 0, hp, prev)
    prev = jnp.where(seg_pos == 0, 0.0, prev)
    nxt = pltpu.roll(h, tm - 1, 0)
    nxt = jnp.where(row == tm - 1, hn, nxt)
    nxt = jnp.where(seg_pos == seg_len - 1, 0.0, nxt)
    xx = 0.5 * (prev + nxt) - h

    def mixed(mi):
        return (h + xx * mix_ref[mi:mi + 1]).astype(BF16)

    vec = vec_ref[...]
    ones_bd = _head_sum_mat()
    ones_bd2 = jnp.concatenate([ones_bd, ones_bd], axis=0)

    def store(o_ref, val):
        for p in range(N_SLAB):
            o_ref[p] = val[:, p * LANES:(p + 1) * LANES].astype(o_ref.dtype)

    xr = mixed(0)
    store(r_o, _dot(xr, wrkv_ref[0]))

    xw = mixed(1)
    tw = jnp.tanh(_dot(xw, w1_ref[...])).astype(BF16)
    neg_e = -math.exp(-0.5)
    store(lw_o.at[0], neg_e * jax.nn.sigmoid(vec[0:1] + _dot(tw, w2f_ref[...])))
    store(lw_o.at[1], neg_e * jax.nn.sigmoid(vec[1:2] + _dot(tw, w2b_ref[...])))

    xg = mixed(5)
    sg = jax.nn.sigmoid(_dot(xg, g1_ref[...])).astype(BF16)
    half = sg.shape[1] // 2
    store(g_o.at[0], _dot(sg[:, :half], g2f_ref[...]))
    store(g_o.at[1], _dot(sg[:, half:], g2b_ref[...]))

    xa = mixed(4)
    a = jax.nn.sigmoid(vec[2:3] + _dot(_dot(xa, a1_ref[...]).astype(BF16), a2_ref[...]))
    store(a_o, a)

    xv = mixed(3)
    v = _dot(xv, wrkv_ref[2])
    if has_vres:
        gate = jax.nn.sigmoid(vec[5:6] + _dot(_dot(xv, v1_ref[...]).astype(BF16), v2_ref[...]))
        for p in range(N_SLAB):
            sl = slice(p * LANES, (p + 1) * LANES)
            vp = v[:, sl]
            v_o[p] = (vp + (vfirst_ref[p].astype(F32) - vp) * gate[:, sl]).astype(v_o.dtype)
    else:
        store(v_o, v)

    xk = mixed(2)
    k = _dot(xk, wrkv_ref[1])
    k_scale = 1.0 + (a - 1.0) * vec[4:5]
    kraw = k * vec[3:4]
    for p in range(N_SLAB):
        sl = slice(p * LANES, (p + 1) * LANES)
        kr = kraw[:, sl]
        ss = _head_sum(kr * kr, ones_bd2)
        kk_o[p] = (kr * lax.rsqrt(jnp.maximum(ss, 1e-24))).astype(kk_o.dtype)
        k_o[p] = (k[:, sl] * k_scale[:, sl]).astype(k_o.dtype)


def _rwkv_features(geom, x, mod, wts, vfirst):
    tm = 256
    R = geom.rows
    has_vres = vfirst is not None
    nblk8 = R // 8
    pm = functools.partial
    in_specs = [
        pl.BlockSpec((tm, D_MODEL), lambda i: (i, 0)),
        pl.BlockSpec((8, D_MODEL), lambda i: (jnp.maximum(i * (tm // 8) - 1, 0), 0)),
        pl.BlockSpec((8, D_MODEL), lambda i: (jnp.minimum((i + 1) * (tm // 8), nblk8 - 1), 0)),
        pl.BlockSpec((1, MOD_ROWS, D_MODEL), lambda i: (geom.seg(i, tm), 0, 0)),
    ]
    args = [x, x, x, mod]
    names = ["norm_g", "mix", "wrkv", "w1", "w2f", "w2b", "g1", "g2f", "g2b", "a1", "a2", "vec"]
    if has_vres:
        names += ["v1", "v2"]
    for nme in names:
        a = wts[nme]
        in_specs.append(_const_spec(a.shape))
        args.append(a)
    pair_spec = pl.BlockSpec((N_SLAB, tm, LANES), lambda i: (0, i, 0))
    dir_spec = pl.BlockSpec((2, N_SLAB, tm, LANES), lambda i: (0, 0, i, 0))
    if has_vres:
        in_specs.append(pair_spec)
        args.append(vfirst)
    pair_shape = jax.ShapeDtypeStruct((N_SLAB, R, LANES), BF16)
    lw_shape = jax.ShapeDtypeStruct((2, N_SLAB, R, LANES), F32)
    g_shape = jax.ShapeDtypeStruct((2, N_SLAB, R, LANES), BF16)
    kern = pm(_rwkv_feat_kernel, tm=tm, seg_lat=geom.L, seg_ctx=geom.C, n_lat=geom.n_lat,
              has_vres=has_vres)
    return pl.pallas_call(
        kern, grid=(R // tm,), in_specs=in_specs,
        out_specs=[pair_spec] * 5 + [dir_spec] * 2, out_shape=[pair_shape] * 5 + [lw_shape, g_shape],
        compiler_params=_cparams(("arbitrary",)),
        name="rwkv_features",
    )(*args)


def _bd_rows(x, lane_lo):
    return jnp.concatenate([jnp.where(lane_lo, x, 0.0), jnp.where(lane_lo, 0.0, x)], axis=0)


def _wkv_kernel(*refs, n_chunk):
    in_refs = (refs[0:6], refs[6:12])
    o_refs = refs[12:14]
    st_ref = refs[14]
    j = pl.program_id(1)
    c = CHUNK

    @pl.when(j == 0)
    def _():
        st_ref[...] = jnp.zeros_like(st_ref)

    t_i = lax.broadcasted_iota(jnp.int32, (c, LANES), 0)
    lane = lax.broadcasted_iota(jnp.int32, (c, LANES), 1)
    s_i = lane % c
    lane_lo = lane < c
    eye = t_i == s_i
    blk = [(t_i // n) == (s_i // n) for n in (8, 16, 32)]
    tc = lax.broadcasted_iota(jnp.int32, (c, 3 * c), 0)
    sc = lax.broadcasted_iota(jnp.int32, (c, 3 * c), 1) % c
    row1 = lax.broadcasted_iota(jnp.int32, (c, 1), 0)
    strict_d = (s_i < t_i, s_i > t_i)
    incl_d = (s_i <= t_i, s_i >= t_i)
    mcum_d = ((sc <= tc).astype(BF16), (sc >= tc).astype(BF16))
    is_last_d = (row1 == c - 1, row1 == 0)
    bd_avg = _head_sum_mat() * (1.0 / HEAD_DIM)
    bd_avg2 = jnp.concatenate([bd_avg, bd_avg], axis=0)
    kr = lax.broadcasted_iota(jnp.int32, (LANES, LANES), 0)
    kc = lax.broadcasted_iota(jnp.int32, (LANES, LANES), 1)
    same_head = (kr // HEAD_DIM) == (kc // HEAD_DIM)
    diag = kr == kc
    eye_f = jnp.where(eye, 1.0, 0.0)

    def bf(x):
        return x.astype(BF16)

    def pmul(x_pair, y_pair):
        return _dot(bf(x_pair), bf(_bd_rows(y_pair, lane_lo)))

    def chunk_body(ci, carry):
        ds_d = (pl.ds(pl.multiple_of(ci * c, c), c),
                pl.ds(pl.multiple_of((n_chunk - 1 - ci) * c, c), c))
        chains = [(d, p) for d in range(2) for p in range(N_SLAB)]
        strict = [strict_d[d] for d, _ in chains]
        incl = [incl_d[d] for d, _ in chains]

        def each(fn, *lists):
            return [fn(*args) for args in zip(*lists)]

        def load(which):
            return [in_refs[d][which][0, p, ds_d[d], :] if which == 5
                    else in_refs[d][which][p, ds_d[d], :].astype(F32) for d, p in chains]

        lw = load(5)
        cum = [_dot(mcum_d[d], jnp.concatenate(_split3(x), axis=0))
               for (d, _), x in zip(chains, lw)]
        e_pos = each(jnp.exp, cum)
        e_neg = each(lambda x: jnp.exp(-x), cum)
        e_exc = each(lambda x, y: jnp.exp(x - y), cum, lw)
        p_end = [jnp.sum(jnp.where(is_last_d[d], e, 0.0), axis=0, keepdims=True)
                 for (d, _), e in zip(chains, e_pos)]
        r_t = each(lambda x, e: x * e, load(0), e_pos)
        kk = load(3)
        a_t = each(lambda x, e: -x * e, kk, e_exc)
        b_t = each(lambda x, y, e: x * y * e, kk, load(4), e_neg)
        k_t = each(lambda x, e: x * e, load(1), e_neg)
        v = load(2)
        gram = each(lambda at, rt, bt, kt: _dot_nt(
            bf(jnp.concatenate([at, rt], axis=0)),
            bf(jnp.concatenate([_bd_rows(bt, lane_lo), _bd_rows(kt, lane_lo)], axis=0))),
            a_t, r_t, b_t, k_t)
        l_ab = each(lambda g, m: jnp.where(m, g[:c, :LANES], 0.0), gram, strict)
        l_ak = each(lambda g, m: jnp.where(m, g[:c, LANES:], 0.0), gram, strict)
        p_rb = each(lambda g, m: jnp.where(m, g[c:, :LANES], 0.0), gram, incl)
        p_rk = each(lambda g, m: jnp.where(m, g[c:, LANES:], 0.0), gram, incl)

        n0 = each(lambda x: jnp.where(blk[0], x, 0.0), l_ab)
        p1 = each(pmul, n0, n0)
        t0 = each(lambda x: eye_f + x, n0)
        both = each(lambda t, q: pmul(jnp.concatenate([t, q], axis=0), q), t0, p1)
        t1 = each(lambda t, bo: t + bo[:c], t0, both)
        tinv = each(lambda t, bo: t + pmul(t, bo[c:]), t1, both)
        prev_blk = blk[0]
        for nb in (blk[1], blk[2], None):
            sel = jnp.logical_not(prev_blk) if nb is None else (nb & jnp.logical_not(prev_blk))
            mid = each(lambda x, t: pmul(jnp.where(sel, x, 0.0), t), l_ab, tinv)
            tinv = each(lambda t, m: t + pmul(t, m), tinv, mid)
            prev_blk = nb

        st = [st_ref[d, p] for d, p in chains]
        st_b = each(bf, st)
        v_bd = each(lambda x: bf(_bd_rows(x, lane_lo)), v)
        rhs_u = each(lambda at, lk, sb, vb: _dot(bf(jnp.concatenate([at, lk], axis=1)),
                                                 jnp.concatenate([sb, vb], axis=0)),
                     a_t, l_ak, st_b, v_bd)
        u_p = each(pmul, tinv, rhs_u)
        o = each(lambda rt, prb, prk, sb, up, vb: _dot(
            bf(jnp.concatenate([rt, prb, prk], axis=1)),
            jnp.concatenate([sb, bf(_bd_rows(up, lane_lo)), vb], axis=0)),
            r_t, p_rb, p_rk, st_b, u_p, v_bd)

        upd = each(lambda bt, kt, pe, up, vv: _dot(
            bf(jnp.concatenate([bt * pe, kt * pe], axis=0).T),
            bf(jnp.concatenate([up, vv], axis=0))),
            b_t, k_t, p_end, u_p, v)
        for i, (d, p) in enumerate(chains):
            p_col = jnp.sum(jnp.where(diag, p_end[i], 0.0), axis=1, keepdims=True)
            st_ref[d, p] = st[i] * p_col + jnp.where(same_head, upd[i], 0.0)

        mu = each(lambda x: _dot(jnp.concatenate(_split2(x), axis=1), bf(bd_avg2)), o)
        dlt = each(lambda x, m: x - m, o, mu)
        var = each(lambda x: _dot(jnp.concatenate(_split2(x * x), axis=1), bf(bd_avg2)), dlt)
        for i, (d, p) in enumerate(chains):
            o_refs[d][p, ds_d[d], :] = (dlt[i] * lax.rsqrt(var[i] + GN_EPS)).astype(o_refs[d].dtype)
        return carry

    lax.fori_loop(0, n_chunk, chunk_body, 0)


def _wkv_scan(geom, r, k, v, kk, a, lw):
    B, L, C = geom.B, geom.L, geom.C
    blk = SCAN_BLOCK
    n_lat_blk = L // blk
    n_ctx_blk = C // blk
    n_steps = n_lat_blk + n_ctx_blk
    lat_blocks = geom.n_lat // blk

    def row_block(d, b, j):
        in_ctx = j < n_ctx_blk
        jc = jnp.where(d == 0, j, n_ctx_blk - 1 - j)
        jl = jnp.where(d == 0, j - n_ctx_blk, n_steps - 1 - j)
        return jnp.where(in_ctx, lat_blocks + b * n_ctx_blk + jc, b * n_lat_blk + jl)

    def specs(d):
        pair = pl.BlockSpec((N_SLAB, blk, LANES), lambda b, j: (0, row_block(d, b, j), 0))
        lw_spec = pl.BlockSpec((1, N_SLAB, blk, LANES), lambda b, j: (d, 0, row_block(d, b, j), 0))
        return pair, lw_spec

    (pair_f, lw_f), (pair_b, lw_b) = specs(0), specs(1)
    kern = functools.partial(_wkv_kernel, n_chunk=blk // CHUNK)
    out_shape = jax.ShapeDtypeStruct((N_SLAB, geom.rows, LANES), BF16)
    return pl.pallas_call(
        kern, grid=(B, n_steps),
        in_specs=[pair_f] * 5 + [lw_f] + [pair_b] * 5 + [lw_b],
        out_specs=[pair_f, pair_b],
        out_shape=[out_shape, out_shape],
        scratch_shapes=[pltpu.VMEM((2, N_SLAB, LANES, LANES), F32)],
        compiler_params=_cparams(("arbitrary", "arbitrary")),
        name="wkv_scan",
    )(r, k, v, kk, a, lw, r, k, v, kk, a, lw)


def _rwkv_out_kernel(onf_ref, onb_ref, r_ref, k_ref, v_ref, g_ref, vec_ref, w_ref, x_ref, mod_ref,
                     o_ref, h_ref):
    j = pl.program_id(1)

    @pl.when(j == 0)
    def _():
        ones_bd = _head_sum_mat()
        ones_bd2 = jnp.concatenate([ones_bd, ones_bd], axis=0)
        vec = vec_ref[...]
        for p in range(N_SLAB):
            sl = slice(p * LANES, (p + 1) * LANES)
            rk, lg, lb = vec[0:1, sl], vec[1:2, sl], vec[2:3, sl]
            f32 = lambda ref, *idx: ref[idx].astype(F32)
            bonus = _head_sum(f32(r_ref, p) * f32(k_ref, p) * rk, ones_bd2) * f32(v_ref, p)
            y = ((f32(onf_ref, p) * lg + lb + bonus) * f32(g_ref, 0, p)
                 + (f32(onb_ref, p) * lg + lb + bonus) * f32(g_ref, 1, p))
            h_ref[:, sl] = y.astype(BF16)

    acc = _dot(h_ref[...], w_ref[...])
    o_ref[...] = x_ref[...] + mod_ref[0][2:3] * acc


def _rwkv_out(geom, on_f, on_b, r, k, v, g, vec, w_o, x, mod):
    tm, tn = 512, 512
    R = geom.rows
    pair_spec = pl.BlockSpec((N_SLAB, tm, LANES), lambda i, j: (0, i, 0))
    dir_spec = pl.BlockSpec((2, N_SLAB, tm, LANES), lambda i, j: (0, 0, i, 0))
    return pl.pallas_call(
        _rwkv_out_kernel, grid=(R // tm, D_MODEL // tn),
        in_specs=[pair_spec, pair_spec, pair_spec, pair_spec, pair_spec, dir_spec,
                  _const_spec(vec.shape),
                  pl.BlockSpec((D_MODEL, tn), lambda i, j: (0, j)),
                  pl.BlockSpec((tm, tn), lambda i, j: (i, j)),
                  pl.BlockSpec((1, MOD_ROWS, tn), lambda i, j: (geom.seg(i, tm), 0, j))],
        out_specs=pl.BlockSpec((tm, tn), lambda i, j: (i, j)),
        out_shape=jax.ShapeDtypeStruct((R, D_MODEL), F32),
        scratch_shapes=[pltpu.VMEM((tm, D_MODEL), BF16)],
        compiler_params=_cparams(("arbitrary", "arbitrary")),
        name="rwkv_out",
    )(on_f, on_b, r, k, v, g, vec, w_o, x, mod)


def _attn_kernel(*refs, tq, tk, n_lat_chunks, lambda_init):
    if n_lat_chunks:
        (q_ref, kc_ref, vc_ref, kl_ref, vl_ref, lam_ref, sg_ref, o_ref,
         m_ref, l_ref, acc_ref, al_ref, sc_ref, p_ref, vtc_ref, vtl_ref) = refs
    else:
        (q_ref, kc_ref, vc_ref, lam_ref, sg_ref, o_ref,
         m_ref, l_ref, acc_ref, al_ref, sc_ref, p_ref, vtc_ref) = refs
    n_ctx = kc_ref.shape[0]

    @pl.when(pl.program_id(2) == 0)
    def _():
        vtc_ref[...] = vc_ref[...].astype(F32).T.astype(BF16)
        for i in range(n_lat_chunks):
            vtl_ref[i] = vl_ref[i * tk:(i + 1) * tk, :].astype(F32).T.astype(BF16)

    lane = lax.broadcasted_iota(jnp.int32, (tq, LANES), 1)
    q = q_ref[...]
    zero = jnp.zeros_like(q)
    qs = (jnp.where(lane < HEAD_DIM, q, zero), jnp.where(lane < HEAD_DIM, zero, q))

    m_ref[...] = jnp.full_like(m_ref, -jnp.inf)
    l_ref[...] = jnp.zeros_like(l_ref)
    acc_ref[...] = jnp.zeros_like(acc_ref)

    n_qt = tq // LANES

    def scores(slot, k):
        nk = k.shape[0]
        for s in range(2):
            res = _dot_nt(k, qs[s])
            for t in range(n_qt):
                sc_ref[slot, s, t, 0:nk, :] = res[:, t * LANES:(t + 1) * LANES]

    def consume(slot, nk, vt):
        for s in range(2):
            for t in range(n_qt):
                cols = slice(t * LANES, (t + 1) * LANES)
                x = sc_ref[slot, s, t, 0:nk, :]
                m_old = m_ref[s, :, cols]
                m_new = jnp.maximum(m_old, jnp.max(x, axis=0, keepdims=True))
                alpha = jnp.exp2(m_old - m_new)
                pr = jnp.exp2(x - m_new)
                l_ref[s, :, cols] = alpha * l_ref[s, :, cols] + jnp.sum(pr, axis=0, keepdims=True)
                m_ref[s, :, cols] = m_new
                al_ref[s, :, cols] = alpha
                p_ref[s, t, 0:nk, :] = pr.astype(BF16)
            pt = jnp.concatenate([p_ref[s, t, 0:nk, :] for t in range(n_qt)], axis=1)
            acc_ref[s] = al_ref[s] * acc_ref[s] + _dot(vt, pt)

    scores(0, kc_ref[...])
    consume(0, n_ctx, vtc_ref[...])

    if n_lat_chunks:
        def kblk(i):
            return kl_ref[pl.ds(pl.multiple_of(i * tk, tk), tk), :]

        def vblk(i):
            return vtl_ref[i]

        scores(0, kblk(0))
        n_pairs = (n_lat_chunks - 1) // 2

        def body(i, carry):
            scores(1, kblk(2 * i + 1))
            consume(0, tk, vblk(2 * i))
            scores(0, kblk(2 * i + 2))
            consume(1, tk, vblk(2 * i + 1))
            return carry

        lax.fori_loop(0, n_pairs, body, 0)
        done = 2 * n_pairs
        if n_lat_chunks - done == 2:
            scores(1, kblk(done + 1))
            consume(0, tk, vblk(done))
            consume(1, tk, vblk(done + 1))
        else:
            consume(0, tk, vblk(done))

    lv = lam_ref[...]
    lam = (jnp.exp(jnp.sum(lv[0:1] * lv[1:2], axis=-1, keepdims=True))
           - jnp.exp(jnp.sum(lv[2:3] * lv[3:4], axis=-1, keepdims=True)) + lambda_init)
    ot = acc_ref[0] / l_ref[0] - lam * (acc_ref[1] / l_ref[1])
    ot = ot * lax.rsqrt(jnp.mean(ot * ot, axis=0, keepdims=True) + SUBLN_EPS)
    o_ref[...] = (ot.T * sg_ref[0:1] * (1.0 - lambda_init)).astype(o_ref.dtype)


def _attention(geom, qkv, lam_vec, subln_g, lambda_init, *, need_ctx):
    B, L, C = geom.B, geom.L, geom.C
    n_heads = D_MODEL // LANES
    ctx_row_blk = geom.n_lat // C
    kc_spec = pl.BlockSpec((C, LANES), lambda b, h, qi: (ctx_row_blk + b, n_heads + h))
    vc_spec = pl.BlockSpec((C, LANES), lambda b, h, qi: (ctx_row_blk + b, 2 * n_heads + h))
    small = [_const_spec(lam_vec.shape), _const_spec(subln_g.shape)]

    def call(name, tq, q_map, n_q, tk, n_lat_chunks, extra_specs, rows_out):
        kern = functools.partial(_attn_kernel, tq=tq, tk=tk, n_lat_chunks=n_lat_chunks,
                                 lambda_init=lambda_init)
        n_kv = 2 + len(extra_specs)
        return pl.pallas_call(
            kern, grid=(B, n_heads, n_q),
            in_specs=[pl.BlockSpec((tq, LANES), q_map), kc_spec, vc_spec] + extra_specs + small,
            out_specs=pl.BlockSpec((tq, LANES), lambda b, h, qi: (q_map(b, h, qi)[0] - rows_out[0], h)),
            out_shape=jax.ShapeDtypeStruct((rows_out[1], D_MODEL), BF16),
            scratch_shapes=[pltpu.VMEM((2, 1, tq), F32),
                            pltpu.VMEM((2, 1, tq), F32),
                            pltpu.VMEM((2, LANES, tq), F32),
                            pltpu.VMEM((2, 1, tq), F32),
                            pltpu.VMEM((2, 2, tq // LANES, max(tk, C), LANES), F32),
                            pltpu.VMEM((2, tq // LANES, max(tk, C), LANES), BF16),
                            pltpu.VMEM((LANES, C), BF16)]
            + ([pltpu.VMEM((n_lat_chunks, LANES, tk), BF16)] if n_lat_chunks else []),
            compiler_params=_cparams(("arbitrary", "arbitrary", "arbitrary")),
            name=name,
        )(*([qkv] * (1 + n_kv) + [lam_vec, subln_g]))

    tq = min(ATTN_TQ, L)
    tk = min(ATTN_TK, L)
    lat_specs = [pl.BlockSpec((L, LANES), lambda b, h, qi: (b, n_heads + h)),
                 pl.BlockSpec((L, LANES), lambda b, h, qi: (b, 2 * n_heads + h))]
    o_lat = call("diff_attention", tq, lambda b, h, qi: (b * (L // tq) + qi, h), L // tq, tk, L // tk,
                 lat_specs, (0, geom.n_lat))
    if not need_ctx:
        return o_lat
    o_ctx = call("diff_attention_ctx", C, lambda b, h, qi: (ctx_row_blk + b, h), 1, tk, 0, [],
                 (ctx_row_blk, geom.n_ctx))
    return jnp.concatenate([o_lat, o_ctx], axis=0)


def _rope_tables(geom):
    L = geom.L
    t = jnp.arange(L, dtype=jnp.int32)
    inv = ROPE_THETA ** (-jnp.arange(ROPE_PAIRS, dtype=F32) / ROPE_PAIRS)
    lane = jnp.arange(LANES, dtype=jnp.int32) % HEAD_DIM
    freq = inv[lane % ROPE_PAIRS]
    pos = jnp.where((lane < HEAD_DIM // 2)[None, :], (t // GRID_W)[:, None], (t % GRID_W)[:, None])
    ang = pos.astype(F32) * freq[None, :]
    sign = jnp.where((lane % 32) < 16, -1.0, 1.0)
    cos_l = jnp.cos(ang)
    sin_l = jnp.sin(ang) * sign[None, :]
    cos = jnp.concatenate([cos_l, jnp.ones((geom.tm, LANES), F32)], axis=0)
    sin = jnp.concatenate([sin_l, jnp.zeros((geom.tm, LANES), F32)], axis=0)
    return cos, sin


def _pad_cols(w, n):
    return jnp.pad(w, ((0, 0), (0, n - w.shape[1])))


def _pad_rows(w, n, offset=0):
    return jnp.pad(w, ((offset, n - offset - w.shape[0]), (0, 0)))


def _rwkv_weights(j, norm_g_row, rw_mix, rw_w_rkv, rw_w0, rw_w1, rw_w2, rw_a0, rw_a1, rw_a2, rw_g1,
                  rw_g2, rw_kk, rw_ka, rw_v0, rw_v1, rw_v2):
    lora_w = rw_w1.shape[-1]
    g_pad = 2 * LANES
    wts = {
        "norm_g": norm_g_row.reshape(1, D_MODEL),
        "mix": jnp.pad(rw_mix[j], ((0, 2), (0, 0))),
        "wrkv": rw_w_rkv[j].astype(BF16),
        "w1": jnp.concatenate([rw_w1[j, 0], rw_w1[j, 1]], axis=1).astype(BF16),
        "w2f": _pad_rows(rw_w2[j, 0], LANES).astype(BF16),
        "w2b": _pad_rows(rw_w2[j, 1], LANES, lora_w).astype(BF16),
        "g1": jnp.concatenate([_pad_cols(rw_g1[j, 0], g_pad), _pad_cols(rw_g1[j, 1], g_pad)],
                              axis=1).astype(BF16),
        "g2f": _pad_rows(rw_g2[j, 0], g_pad).astype(BF16),
        "g2b": _pad_rows(rw_g2[j, 1], g_pad).astype(BF16),
        "a1": _pad_cols(rw_a1[j], LANES).astype(BF16),
        "a2": _pad_rows(rw_a2[j], LANES).astype(BF16),
    }
    v0 = rw_v0[j - 1] if j > 0 else jnp.zeros((D_MODEL,), F32)
    wts["vec"] = jnp.stack([rw_w0[j, 0], rw_w0[j, 1], rw_a0[j], rw_kk[j], rw_ka[j], v0,
                            jnp.zeros((D_MODEL,), F32), jnp.zeros((D_MODEL,), F32)], axis=0)
    if j > 0:
        wts["v1"] = _pad_cols(rw_v1[j - 1], LANES).astype(BF16)
        wts["v2"] = _pad_rows(rw_v2[j - 1], LANES).astype(BF16)
    return wts


def kernel(x, c, ctx, c_ctx, ada_w, ada_b, norm_g, final_g, rw_mix, rw_w_rkv, rw_w0, rw_w1, rw_w2, rw_a0, rw_a1, rw_a2, rw_g1, rw_g2, rw_kk, rw_ka, rw_rk, rw_ln_g, rw_ln_b, rw_w_o, rw_v0, rw_v1, rw_v2, da_w_qkv, da_w_o, da_lq1, da_lk1, da_lq2, da_lk2, da_subln_g, mlp_w1, mlp_w2):
    B, L, D = x.shape
    C = ctx.shape[1]
    depth = ada_w.shape[0]
    geom = _Geom(B, L, C)
    tm = geom.tm

    n_seg = 16
    cc = jnp.concatenate([c, c_ctx[None, :], jnp.zeros((n_seg - B - 1, D), F32)], axis=0)
    mod_all = _ada_table(cc, ada_w, ada_b).reshape(depth, n_seg, 6, D)
    mod_all = jnp.pad(mod_all, ((0, 0), (0, 0), (0, MOD_ROWS - 6), (0, 0)))

    xs = jnp.concatenate([x.reshape(B * L, D), ctx.reshape(B * C, D)], axis=0)
    rope_cos, rope_sin = _rope_tables(geom)
    vfirst = None

    for i in range(depth):
        last = i == depth - 1
        mod = mod_all[i]
        j = i // 2
        ng0 = norm_g[i, 0].reshape(1, D)
        ng1 = norm_g[i, 1].reshape(1, D)
        n_rows = geom.n_lat if last else geom.rows
        tn = 512
        xtile = pl.BlockSpec((tm, tn), lambda ti, tj: (ti, tj))
        if i % 2 == 0:
            wts = _rwkv_weights(j, norm_g[i, 0], rw_mix, rw_w_rkv, rw_w0, rw_w1, rw_w2, rw_a0, rw_a1,
                                rw_a2, rw_g1, rw_g2, rw_kk, rw_ka, rw_v0, rw_v1, rw_v2)
            r, k, v, kk, a, lw, g = _rwkv_features(geom, xs, mod, wts, vfirst)
            if j == 0:
                vfirst = v
            on_f, on_b = _wkv_scan(geom, r, k, v, kk, a, lw)
            vec = jnp.stack([rw_rk[j].reshape(D), rw_ln_g[j], rw_ln_b[j]]
                            + [jnp.zeros((D,), F32)] * 5, axis=0)
            xs = _rwkv_out(geom, on_f, on_b, r, k, v, g, vec, rw_w_o[j].astype(BF16), xs, mod)
        else:
            lambda_init = 0.8 - 0.6 * math.exp(-0.3 * i)
            seq_tiles = L // tm
            rope_spec = pl.BlockSpec(
                (tm, LANES),
                lambda ti, tj: (jnp.where(ti < B * seq_tiles, ti % seq_tiles, seq_tiles), 0))
            qkv = _mm_call("attn_qkv", xs, da_w_qkv[j].astype(BF16), n_rows=geom.rows, tm=tm, tn=2 * tn,
                           prologue=_normmod_prologue(0, 1),
                           pro_inputs=[(mod, _mod_full_spec(geom, tm)), (ng0, _const_spec(ng0.shape))],
                           epilogue=_rope_epilogue(D // tn, 2 * tn),
                           epi_inputs=[(rope_cos, rope_spec), (rope_sin, rope_spec)],
                           out_dtype=BF16)
            lam_vec = jnp.pad(jnp.stack([da_lq1[j], da_lk1[j], da_lq2[j], da_lk2[j]], axis=0),
                              ((0, 4), (0, LANES - HEAD_DIM)))
            sg = da_subln_g[j].reshape(1, LANES)
            o_att = _attention(geom, qkv, lam_vec, sg, lambda_init, need_ctx=not last)
            xs = _mm_call("attn_out", o_att, da_w_o[j].astype(BF16), n_rows=n_rows, tm=tm, tn=D,
                          prologue=None, pro_inputs=[], epilogue=_gated_residual_epilogue(2),
                          epi_inputs=[(xs, pl.BlockSpec((tm, D), lambda ti, tj: (ti, 0))),
                                      (mod, _mod_tile_spec(geom, tm, D))],
                          out_dtype=F32)
        hmid = _mm_call("mlp_up", xs, mlp_w1[i].astype(BF16), n_rows=n_rows, tm=tm, tn=2 * tn,
                        prologue=_normmod_prologue(3, 4),
                        pro_inputs=[(mod, _mod_full_spec(geom, tm)), (ng1, _const_spec(ng1.shape))],
                        epilogue=_relu2_epilogue, epi_inputs=[], out_dtype=BF16)
        if last:
            tmf = min(tm, 512)
            fg = final_g.reshape(1, D)
            xs = _mm_call("mlp_down_final", hmid, mlp_w2[i].astype(BF16), n_rows=n_rows, tm=tmf, tn=D,
                          prologue=None, pro_inputs=[], epilogue=_gated_residual_norm_epilogue(5),
                          epi_inputs=[(xs, pl.BlockSpec((tmf, D), lambda ti, tj: (ti, 0))),
                                      (mod, _mod_tile_spec(geom, tmf, D)),
                                      (fg, _const_spec(fg.shape))],
                          out_dtype=F32)
        else:
            xs = _mm_call("mlp_down", hmid, mlp_w2[i].astype(BF16), n_rows=n_rows, tm=tm, tn=tn,
                          prologue=None, pro_inputs=[], epilogue=_gated_residual_epilogue(5),
                          epi_inputs=[(xs, xtile), (mod, _mod_tile_spec(geom, tm, tn))],
                          out_dtype=F32)
    return xs[:B * L].reshape(B, L, D)
```
